```python
import jax, jax.numpy as jnp
from jax import lax
import numpy as np

D_MODEL = 2048
BATCH = 2
SEQ = 16384
DEPTH = 2

HEAD_DIM = 128
N_Q_HEADS = 8
N_KV_HEADS = 2
GQA_GROUP = N_Q_HEADS // N_KV_HEADS
N_MEM_HEADS = 4
MEM_TOKENS = 256
SGU_GROUPS = 4
SGU_CHUNK = 128
Q_BLOCK = 128
GRID_W = 64
ROPE_THETA = 10000.0
N_BRANCH = 3
EPS = 1e-6
D_FF = -(-8 * D_MODEL // (3 * 256)) * 256

SGU_WIDTH = SGU_GROUPS * HEAD_DIM
Q_WIDTH = N_Q_HEADS * HEAD_DIM
KV_WIDTH = N_KV_HEADS * HEAD_DIM
MEM_WIDTH = N_MEM_HEADS * HEAD_DIM
GATE_WIDTH = N_BRANCH * D_MODEL
IN_WIDTH = 2 * SGU_WIDTH + Q_WIDTH + 2 * KV_WIDTH + MEM_WIDTH + GATE_WIDTH
SPLIT_IDX = [2 * SGU_WIDTH,
             2 * SGU_WIDTH + Q_WIDTH,
             2 * SGU_WIDTH + Q_WIDTH + KV_WIDTH,
             2 * SGU_WIDTH + Q_WIDTH + 2 * KV_WIDTH,
             2 * SGU_WIDTH + Q_WIDTH + 2 * KV_WIDTH + MEM_WIDTH]

kernel_name = "hybrid_gated_sgu_gqa_memxattn_encoder"


def rms_norm(x, g):
    xf = x.astype(jnp.float32)
    y = xf * lax.rsqrt(jnp.mean(xf * xf, axis=-1, keepdims=True) + EPS)
    return (y * g.astype(jnp.float32)).astype(x.dtype)


def axial_rope_tables(rows):
    half = HEAD_DIM // 2
    inv = ROPE_THETA ** (-jnp.arange(0, half, 2, dtype=jnp.float32) / half)
    r = jnp.repeat(jnp.arange(rows, dtype=jnp.float32), GRID_W)
    c = jnp.tile(jnp.arange(GRID_W, dtype=jnp.float32), rows)
    ar = r[:, None] * inv[None, :]
    ac = c[:, None] * inv[None, :]
    return (jnp.cos(ar), jnp.sin(ar), jnp.cos(ac), jnp.sin(ac))


def rotate(x, cos, sin):
    x1, x2 = jnp.split(x, 2, axis=-1)
    cos = cos.astype(x.dtype)
    sin = sin.astype(x.dtype)
    return jnp.concatenate([x1 * cos - x2 * sin, x2 * cos + x1 * sin], axis=-1)


def apply_axial_rope(x, tabs):
    cos_r, sin_r, cos_c, sin_c = tabs
    half = HEAD_DIM // 2
    return jnp.concatenate([rotate(x[..., :half], cos_r, sin_r),
                            rotate(x[..., half:], cos_c, sin_c)], axis=-1)


def block_gqa(q, k, v):
    B, KV, G, S, Dh = q.shape
    nb = S // Q_BLOCK
    qb = q.reshape(B, KV, G, nb, Q_BLOCK, Dh).transpose(3, 0, 1, 2, 4, 5)
    scale = Dh ** -0.5

    def one_block(qblk):
        s = jnp.einsum('bkgqd,bksd->bkgqs', qblk, k).astype(jnp.float32) * scale
        p = jax.nn.softmax(s, axis=-1).astype(v.dtype)
        return jnp.einsum('bkgqs,bksd->bkgqd', p, v)

    o = lax.map(one_block, qb)
    return o.transpose(1, 2, 3, 0, 4, 5).reshape(B, KV, G, S, Dh)


def hybrid_layer(x, mem, tabs, norm_mix, w_in, gate_b, sgu_norm, sgu_w, sgu_b,
                 q_norm, k_norm, mem_norm, w_mem_kv, mq_norm, mk_norm,
                 w_br_sgu, w_br_attn, w_br_mem, w_out, norm_ffn, w_gate_up, w_down):
    B, S, D = x.shape
    M = mem.shape[1]
    h = rms_norm(x, norm_mix)
    z = h @ w_in
    z_sgu, zq, zk, zv, zmq, zg = jnp.split(z, SPLIT_IDX, axis=-1)

    zs = jax.nn.gelu(z_sgu, approximate=False)
    u, vs = jnp.split(zs, 2, axis=-1)
    vs = rms_norm(vs.reshape(B, S, SGU_GROUPS, HEAD_DIM), sgu_norm)
    vc = vs.reshape(B, S // SGU_CHUNK, SGU_CHUNK, SGU_GROUPS, HEAD_DIM)
    sp = jnp.einsum('gpq,bnqgc->bnpgc', sgu_w, vc) + sgu_b.T[None, None, :, :, None]
    y_sgu = u * sp.reshape(B, S, SGU_WIDTH)

    q = zq.reshape(B, S, N_Q_HEADS, HEAD_DIM).transpose(0, 2, 1, 3)
    k = zk.reshape(B, S, N_KV_HEADS, HEAD_DIM).transpose(0, 2, 1, 3)
    v = zv.reshape(B, S, N_KV_HEADS, HEAD_DIM).transpose(0, 2, 1, 3)
    q = apply_axial_rope(rms_norm(q, q_norm), tabs)
    k = apply_axial_rope(rms_norm(k, k_norm), tabs)
    o = block_gqa(q.reshape(B, N_KV_HEADS, GQA_GROUP, S, HEAD_DIM), k, v)
    y_attn = o.reshape(B, N_Q_HEADS, S, HEAD_DIM).transpose(0, 2, 1, 3).reshape(B, S, Q_WIDTH)

    mkv = rms_norm(mem, mem_norm) @ w_mem_kv
    mk, mv = jnp.split(mkv, 2, axis=-1)
    mk = rms_norm(mk.reshape(B, M, N_MEM_HEADS, HEAD_DIM), mk_norm).transpose(0, 2, 1, 3)
    mv = mv.reshape(B, M, N_MEM_HEADS, HEAD_DIM).transpose(0, 2, 1, 3)
    mq = rms_norm(zmq.reshape(B, S, N_MEM_HEADS, HEAD_DIM), mq_norm).transpose(0, 2, 1, 3)
    ms = jnp.einsum('bhqd,bhmd->bhqm', mq, mk).astype(jnp.float32) * (HEAD_DIM ** -0.5)
    mp = jax.nn.softmax(ms, axis=-1).astype(mv.dtype)
    mo = jnp.einsum('bhqm,bhmd->bhqd', mp, mv)
    y_mem = mo.transpose(0, 2, 1, 3).reshape(B, S, MEM_WIDTH)

    g = jax.nn.sigmoid((zg.reshape(B, S, N_BRANCH, D) + gate_b).astype(jnp.float32)).astype(x.dtype)
    y = (g[:, :, 0] * (y_sgu @ w_br_sgu)
         + g[:, :, 1] * (y_attn @ w_br_attn)
         + g[:, :, 2] * (y_mem @ w_br_mem))
    x = x + y @ w_out

    a, b = jnp.split(rms_norm(x, norm_ffn) @ w_gate_up, 2, axis=-1)
    return x + (jax.nn.silu(a) * b) @ w_down


def setup_inputs(seed: int = 0) -> dict:
    key = jax.random.key(seed)
    ks = jax.random.split(key, 24)
    f32 = jnp.float32

    def nrm(k, shape, fan_in):
        return jax.random.normal(k, shape, f32) * (fan_in ** -0.5)

    def gain(k, shape):
        return 1.0 + 0.05 * jax.random.normal(k, shape, f32)

    L = DEPTH
    return {
        "x": jax.random.normal(ks[0], (BATCH, SEQ, D_MODEL), f32),
        "mem": jax.random.normal(ks[1], (BATCH, MEM_TOKENS, D_MODEL), f32),
        "norm_mix": gain(ks[2], (L, D_MODEL)),
        "w_in": nrm(ks[3], (L, D_MODEL, IN_WIDTH), D_MODEL),
        "gate_b": 0.01 * jax.random.normal(ks[4], (L, N_BRANCH, D_MODEL), f32),
        "sgu_norm": gain(ks[5], (L, HEAD_DIM)),
        "sgu_w": nrm(ks[6], (L, SGU_GROUPS, SGU_CHUNK, SGU_CHUNK), SGU_CHUNK),
        "sgu_b": 1.0 + 0.05 * jax.random.normal(ks[7], (L, SGU_GROUPS, SGU_CHUNK), f32),
        "q_norm": gain(ks[8], (L, HEAD_DIM)),
        "k_norm": gain(ks[9], (L, HEAD_DIM)),
        "mem_norm": gain(ks[10], (L, D_MODEL)),
        "w_mem_kv": nrm(ks[11], (L, D_MODEL, 2 * MEM_WIDTH), D_MODEL),
        "mq_norm": gain(ks[12], (L, HEAD_DIM)),
        "mk_norm": gain(ks[13], (L, HEAD_DIM)),
        "w_br_sgu": nrm(ks[14], (L, SGU_WIDTH, D_MODEL), SGU_WIDTH),
        "w_br_attn": nrm(ks[15], (L, Q_WIDTH, D_MODEL), Q_WIDTH),
        "w_br_mem": nrm(ks[16], (L, MEM_WIDTH, D_MODEL), MEM_WIDTH),
        "w_out": nrm(ks[17], (L, D_MODEL, D_MODEL), D_MODEL),
        "norm_ffn": gain(ks[18], (L, D_MODEL)),
        "w_gate_up": nrm(ks[19], (L, D_MODEL, 2 * D_FF), D_MODEL),
        "w_down": nrm(ks[20], (L, D_FF, D_MODEL), D_FF),
    }


def reference(x, mem, norm_mix, w_in, gate_b, sgu_norm, sgu_w, sgu_b, q_norm, k_norm,
              mem_norm, w_mem_kv, mq_norm, mk_norm, w_br_sgu, w_br_attn, w_br_mem,
              w_out, norm_ffn, w_gate_up, w_down):
    rows = x.shape[1] // GRID_W
    tabs = axial_rope_tables(rows)
    for l in range(DEPTH):
        x = hybrid_layer(x, mem, tabs, norm_mix[l], w_in[l], gate_b[l], sgu_norm[l], sgu_w[l],
                         sgu_b[l], q_norm[l], k_norm[l], mem_norm[l], w_mem_kv[l], mq_norm[l],
                         mk_norm[l], w_br_sgu[l], w_br_attn[l], w_br_mem[l], w_out[l],
                         norm_ffn[l], w_gate_up[l], w_down[l])
    return x
```

```python
import functools
import math

import jax
import jax.numpy as jnp
from jax import lax
from jax.experimental import pallas as pl
from jax.experimental.pallas import tpu as pltpu

HEAD_DIM = 128
N_Q_HEADS = 8
N_KV_HEADS = 2
GQA_GROUP = N_Q_HEADS // N_KV_HEADS
N_MEM_HEADS = 4
SGU_GROUPS = 4
SGU_CHUNK = 128
GRID_W = 64
ROPE_THETA = 10000.0
N_BRANCH = 3
EPS = 1e-6

SGU_WIDTH = SGU_GROUPS * HEAD_DIM
Q_WIDTH = N_Q_HEADS * HEAD_DIM
KV_WIDTH = N_KV_HEADS * HEAD_DIM
MEM_WIDTH = N_MEM_HEADS * HEAD_DIM
MIX_WIDTH = 2 * SGU_WIDTH + Q_WIDTH + 2 * KV_WIDTH + MEM_WIDTH
OFF_Q = 2 * SGU_WIDTH
OFF_K = OFF_Q + Q_WIDTH
OFF_V = OFF_K + KV_WIDTH
OFF_MQ = OFF_V + KV_WIDTH

LOG2E = math.log2(math.e)
ATTN_SCALE = HEAD_DIM ** -0.5

V7X_VMEM_LIMIT_BYTES = 56 * 1024 * 1024

BF16 = jnp.bfloat16
F32 = jnp.float32


def _dot(a, b):
    return jnp.dot(a, b, preferred_element_type=F32)


def _dot_nt(a, b):
    return lax.dot_general(a, b, (((1,), (1,)), ((), ())), preferred_element_type=F32)


def _rms(z, gain):
    return z * lax.rsqrt(jnp.mean(z * z, axis=-1, keepdims=True) + EPS) * gain


def _gelu(z):
    return 0.5 * z * (1.0 + lax.erf(z * math.sqrt(0.5)))


def _rope(z, cos, sin_signed, first_of_pair):
    partner = jnp.where(first_of_pair, pltpu.roll(z, 96, 1), pltpu.roll(z, 32, 1))
    return z * cos + partner * sin_signed


def _params(sem):
    return pltpu.CompilerParams(dimension_semantics=sem, vmem_limit_bytes=V7X_VMEM_LIMIT_BYTES)


def _memkv_kernel(mem_ref, mnorm_ref, w_ref, mkn_ref, mk_ref, mv_ref):
    hm = _rms(mem_ref[...], mnorm_ref[...]).astype(BF16)
    kv = _dot(hm, w_ref[...])
    for hh in range(N_MEM_HEADS):
        sl = slice(hh * HEAD_DIM, (hh + 1) * HEAD_DIM)
        mk_ref[:, sl] = _rms(kv[:, sl], mkn_ref[...]).astype(BF16)
    mv_ref[...] = kv[:, MEM_WIDTH:].astype(BF16)


def _memkv(mem2d, mem_norm, w_mem_kv, mk_norm, n_mem):
    rows, d = mem2d.shape
    return pl.pallas_call(
        _memkv_kernel,
        grid=(rows // n_mem,),
        in_specs=[
            pl.BlockSpec((n_mem, d), lambda i: (i, 0)),
            pl.BlockSpec((1, d), lambda i: (0, 0)),
            pl.BlockSpec((d, 2 * MEM_WIDTH), lambda i: (0, 0)),
            pl.BlockSpec((1, HEAD_DIM), lambda i: (0, 0)),
        ],
        out_specs=[
            pl.BlockSpec((n_mem, MEM_WIDTH), lambda i: (i, 0)),
            pl.BlockSpec((n_mem, MEM_WIDTH), lambda i: (i, 0)),
        ],
        out_shape=[jax.ShapeDtypeStruct((rows, MEM_WIDTH), BF16)] * 2,
        compiler_params=_params(("arbitrary",)),
        name="memkv",
    )(mem2d, mem_norm, w_mem_kv, mk_norm)


def _inproj_kernel(x_ref, nmix_ref, w_ref, cos_ref, sin_ref, sgun_ref, sguw_ref, sgub_ref,
                   qn_ref, kn_ref, mqn_ref, mk_ref, mv_ref,
                   ysgu_ref, q_ref, k_ref, v_ref, ymem_ref):
    tm = x_ref.shape[0]
    h = _rms(x_ref[...], nmix_ref[...]).astype(BF16)

    zs = _gelu(_dot(h, w_ref[:, 0:OFF_Q]))
    for g in range(SGU_GROUPS):
        sl = slice(g * HEAD_DIM, (g + 1) * HEAD_DIM)
        u_g = zs[:, sl]
        vs_g = _rms(zs[:, SGU_WIDTH + g * HEAD_DIM:SGU_WIDTH + (g + 1) * HEAD_DIM],
                    sgun_ref[...]).astype(BF16)
        w_g = sguw_ref[g]
        for c in range(tm // SGU_CHUNK):
            rows = slice(c * SGU_CHUNK, (c + 1) * SGU_CHUNK)
            sp = _dot(w_g, vs_g[rows, :]) + sgub_ref[:, sl]
            ysgu_ref[rows, sl] = (u_g[rows, :] * sp).astype(BF16)

    cos = cos_ref[...]
    sin_signed = sin_ref[...]
    lane = lax.broadcasted_iota(jnp.int32, (tm, HEAD_DIM), 1)
    first_of_pair = (lane & 32) == 0
    zq = _dot(h, w_ref[:, OFF_Q:OFF_K])
    for hh in range(N_Q_HEADS):
        sl = slice(hh * HEAD_DIM, (hh + 1) * HEAD_DIM)
        qh = _rope(_rms(zq[:, sl], qn_ref[...]), cos, sin_signed, first_of_pair)
        q_ref[:, sl] = (qh * (ATTN_SCALE * LOG2E)).astype(BF16)
    zkv = _dot(h, w_ref[:, OFF_K:OFF_MQ])
    for hh in range(N_KV_HEADS):
        sl = slice(hh * HEAD_DIM, (hh + 1) * HEAD_DIM)
        k_ref[:, sl] = _rope(_rms(zkv[:, sl], kn_ref[...]), cos, sin_signed, first_of_pair).astype(BF16)
    v_ref[...] = zkv[:, KV_WIDTH:].astype(BF16)

    zmq = _dot(h, w_ref[:, OFF_MQ:MIX_WIDTH])
    for hh in range(N_MEM_HEADS):
        sl = slice(hh * HEAD_DIM, (hh + 1) * HEAD_DIM)
        mq = (_rms(zmq[:, sl], mqn_ref[...]) * (ATTN_SCALE * LOG2E)).astype(BF16)
        s = _dot_nt(mq, mk_ref[:, sl])
        p = jnp.exp2(s - jnp.max(s, axis=-1, keepdims=True))
        o = _dot(p.astype(BF16), mv_ref[:, sl])
        ymem_ref[:, sl] = (o / jnp.sum(p, axis=-1, keepdims=True)).astype(BF16)


def _inproj(x2d, norm_mix, w_in, cos_t, sin_t, sgu_norm, sgu_w, sgu_b_full, q_norm, k_norm,
            mq_norm, mk, mv, seq, n_mem, tm):
    t, d = x2d.shape
    tiles_per_seq = seq // tm
    const = lambda i: (0, 0)
    row = lambda i: (i, 0)
    return pl.pallas_call(
        _inproj_kernel,
        grid=(t // tm,),
        in_specs=[
            pl.BlockSpec((tm, d), row),
            pl.BlockSpec((1, d), const),
            pl.BlockSpec((d, MIX_WIDTH), const),
            pl.BlockSpec((tm, HEAD_DIM), lambda i: (i % tiles_per_seq, 0)),
            pl.BlockSpec((tm, HEAD_DIM), lambda i: (i % tiles_per_seq, 0)),
            pl.BlockSpec((1, HEAD_DIM), const),
            pl.BlockSpec((SGU_GROUPS, SGU_CHUNK, SGU_CHUNK), lambda i: (0, 0, 0)),
            pl.BlockSpec((SGU_CHUNK, SGU_WIDTH), const),
            pl.BlockSpec((1, HEAD_DIM), const),
            pl.BlockSpec((1, HEAD_DIM), const),
            pl.BlockSpec((1, HEAD_DIM), const),
            pl.BlockSpec((n_mem, MEM_WIDTH), lambda i: (i // tiles_per_seq, 0)),
            pl.BlockSpec((n_mem, MEM_WIDTH), lambda i: (i // tiles_per_seq, 0)),
        ],
        out_specs=[
            pl.BlockSpec((tm, SGU_WIDTH), row),
            pl.BlockSpec((tm, Q_WIDTH), row),
            pl.BlockSpec((tm, KV_WIDTH), row),
            pl.BlockSpec((tm, KV_WIDTH), row),
            pl.BlockSpec((tm, MEM_WIDTH), row),
        ],
        out_shape=[
            jax.ShapeDtypeStruct((t, SGU_WIDTH), BF16),
            jax.ShapeDtypeStruct((t, Q_WIDTH), BF16),
            jax.ShapeDtypeStruct((t, KV_WIDTH), BF16),
            jax.ShapeDtypeStruct((t, KV_WIDTH), BF16),
            jax.ShapeDtypeStruct((t, MEM_WIDTH), BF16),
        ],
        compiler_params=_params(("arbitrary",)),
        name="inproj",
    )(x2d, norm_mix, w_in, cos_t, sin_t, sgu_norm, sgu_w, sgu_b_full, q_norm, k_norm, mq_norm, mk, mv)


def _flash_kernel(q_ref, k_ref, v_ref, o_ref, m_sc, l_sc, acc_sc):
    ki = pl.program_id(3)

    @pl.when(ki == 0)
    def _():
        m_sc[...] = jnp.full(m_sc.shape, -jnp.inf, F32)
        l_sc[...] = jnp.zeros(l_sc.shape, F32)
        acc_sc[...] = jnp.zeros(acc_sc.shape, F32)

    k = k_ref[...]
    v = v_ref[...]
    for hh in range(GQA_GROUP):
        sl = slice(hh * HEAD_DIM, (hh + 1) * HEAD_DIM)
        s = _dot_nt(q_ref[:, sl], k)
        m_prev = m_sc[hh]
        m_new = jnp.maximum(m_prev, jnp.max(s, axis=-1, keepdims=True))
        alpha = jnp.exp2(m_prev - m_new)
        p = jnp.exp2(s - m_new[:, 0:1])
        l_sc[hh] = alpha * l_sc[hh] + jnp.sum(p, axis=-1, keepdims=True)
        acc_sc[hh] = alpha * acc_sc[hh] + _dot(p.astype(BF16), v)
        m_sc[hh] = m_new

    @pl.when(ki == pl.num_programs(3) - 1)
    def _():
        for hh in range(GQA_GROUP):
            sl = slice(hh * HEAD_DIM, (hh + 1) * HEAD_DIM)
            o_ref[:, sl] = (acc_sc[hh] / l_sc[hh]).astype(BF16)


def _flash(q, k, v, batch, seq, tq, tk):
    t = q.shape[0]
    nq = seq // tq
    nk = seq // tk
    group_w = GQA_GROUP * HEAD_DIM
    return pl.pallas_call(
        _flash_kernel,
        grid=(batch, N_KV_HEADS, nq, nk),
        in_specs=[
            pl.BlockSpec((tq, group_w), lambda b, g, qi, ki: (b * nq + qi, g)),
            pl.BlockSpec((tk, HEAD_DIM), lambda b, g, qi, ki: (b * nk + ki, g)),
            pl.BlockSpec((tk, HEAD_DIM), lambda b, g, qi, ki: (b * nk + ki, g)),
        ],
        out_specs=pl.BlockSpec((tq, group_w), lambda b, g, qi, ki: (b * nq + qi, g)),
        out_shape=jax.ShapeDtypeStruct((t, Q_WIDTH), BF16),
        scratch_shapes=[
            pltpu.VMEM((GQA_GROUP, tq, HEAD_DIM), F32),
            pltpu.VMEM((GQA_GROUP, tq, HEAD_DIM), F32),
            pltpu.VMEM((GQA_GROUP, tq, HEAD_DIM), F32),
        ],
        compiler_params=_params(("arbitrary", "arbitrary", "arbitrary", "arbitrary")),
        name="flash",
    )(q, k, v)


def _merge_kernel(x_ref, nmix_ref, ysgu_ref, yattn_ref, ymem_ref, wg0_ref, wg1_ref, wg2_ref,
                  gb_ref, wbs_ref, wba_ref, wbm_ref, wout_ref, o_ref, h_sc):
    j = pl.program_id(1)

    @pl.when(j == 0)
    def _():
        x = x_ref[...]
        h_sc[...] = _rms(x, nmix_ref[...]).astype(BF16)
        o_ref[...] = x

    h = h_sc[...]
    y = jax.nn.sigmoid(_dot(h, wg0_ref[...]) + gb_ref[0:1, :]) * _dot(ysgu_ref[...], wbs_ref[...])
    y += jax.nn.sigmoid(_dot(h, wg1_ref[...]) + gb_ref[1:2, :]) * _dot(yattn_ref[...], wba_ref[...])
    y += jax.nn.sigmoid(_dot(h, wg2_ref[...]) + gb_ref[2:3, :]) * _dot(ymem_ref[...], wbm_ref[...])
    o_ref[...] += _dot(y.astype(BF16), wout_ref[...])


def _merge(x2d, norm_mix, y_sgu, y_attn, y_mem, w_in, gate_b, w_br_sgu, w_br_attn, w_br_mem,
           w_out, tm, tn):
    t, d = x2d.shape
    nj = d // tn
    gate0 = MIX_WIDTH // tn
    row = lambda i, j: (i, 0)
    col = lambda i, j: (0, j)
    return pl.pallas_call(
        _merge_kernel,
        grid=(t // tm, nj),
        in_specs=[
            pl.BlockSpec((tm, d), row),
            pl.BlockSpec((1, d), lambda i, j: (0, 0)),
            pl.BlockSpec((tm, SGU_WIDTH), row),
            pl.BlockSpec((tm, Q_WIDTH), row),
            pl.BlockSpec((tm, MEM_WIDTH), row),
            pl.BlockSpec((d, tn), lambda i, j: (0, gate0 + j)),
            pl.BlockSpec((d, tn), lambda i, j: (0, gate0 + nj + j)),
            pl.BlockSpec((d, tn), lambda i, j: (0, gate0 + 2 * nj + j)),
            pl.BlockSpec((N_BRANCH, tn), col),
            pl.BlockSpec((SGU_WIDTH, tn), col),
            pl.BlockSpec((Q_WIDTH, tn), col),
            pl.BlockSpec((MEM_WIDTH, tn), col),
            pl.BlockSpec((tn, d), lambda i, j: (j, 0)),
        ],
        out_specs=pl.BlockSpec((tm, d), row),
        out_shape=jax.ShapeDtypeStruct((t, d), F32),
        scratch_shapes=[pltpu.VMEM((tm, d), BF16)],
        compiler_params=_params(("arbitrary", "arbitrary")),
        name="merge",
    )(x2d, norm_mix, y_sgu, y_attn, y_mem, w_in, w_in, w_in, gate_b, w_br_sgu, w_br_attn,
      w_br_mem, w_out)


def _ffn_kernel(x_ref, nffn_ref, wg_ref, wu_ref, wd_ref, o_ref, h_sc):
    j = pl.program_id(1)

    @pl.when(j == 0)
    def _():
        x = x_ref[...]
        h_sc[...] = _rms(x, nffn_ref[...]).astype(BF16)
        o_ref[...] = x

    h = h_sc[...]
    a = _dot(h, wg_ref[...])
    b = _dot(h, wu_ref[...])
    o_ref[...] += _dot((a * jax.nn.sigmoid(a) * b).astype(BF16), wd_ref[...])


def _ffn(x2d, norm_ffn, w_gate_up, w_down, tm, tf):
    t, d = x2d.shape
    d_ff = w_down.shape[0]
    nj = d_ff // tf
    row = lambda i, j: (i, 0)
    return pl.pallas_call(
        _ffn_kernel,
        grid=(t // tm, nj),
        in_specs=[
            pl.BlockSpec((tm, d), row),
            pl.BlockSpec((1, d), lambda i, j: (0, 0)),
            pl.BlockSpec((d, tf), lambda i, j: (0, j)),
            pl.BlockSpec((d, tf), lambda i, j: (0, nj + j)),
            pl.BlockSpec((tf, d), lambda i, j: (j, 0)),
        ],
        out_specs=pl.BlockSpec((tm, d), row),
        out_shape=jax.ShapeDtypeStruct((t, d), F32),
        scratch_shapes=[pltpu.VMEM((tm, d), BF16)],
        compiler_params=_params(("arbitrary", "arbitrary")),
        name="ffn",
    )(x2d, norm_ffn, w_gate_up, w_gate_up, w_down)


def _rope_tables(seq):
    quarter = HEAD_DIM // 4
    inv = ROPE_THETA ** (-jnp.arange(0, 2 * quarter, 2, dtype=F32) / (2 * quarter))
    pos = jnp.arange(seq, dtype=jnp.int32)
    ar = (pos // GRID_W).astype(F32)[:, None] * inv[None, :]
    ac = (pos % GRID_W).astype(F32)[:, None] * inv[None, :]
    cos_t = jnp.concatenate([jnp.cos(ar), jnp.cos(ar), jnp.cos(ac), jnp.cos(ac)], axis=-1)
    sin_t = jnp.concatenate([-jnp.sin(ar), jnp.sin(ar), -jnp.sin(ac), jnp.sin(ac)], axis=-1)
    return cos_t, sin_t


def _tile(n, want):
    if n <= want:
        return n
    for cand in range(want, 0, -128):
        if n % cand == 0:
            return cand
    return n


def kernel(x, mem, norm_mix, w_in, gate_b, sgu_norm, sgu_w, sgu_b, q_norm, k_norm, mem_norm,
           w_mem_kv, mq_norm, mk_norm, w_br_sgu, w_br_attn, w_br_mem, w_out, norm_ffn,
           w_gate_up, w_down):
    batch, seq, d = x.shape
    n_mem = mem.shape[1]
    depth = w_in.shape[0]
    assert seq % SGU_CHUNK == 0 and seq % GRID_W == 0

    tm = _tile(seq, 512)
    tq = _tile(seq, 512)
    tk = _tile(seq, 1024)
    tn = _tile(d, 512)
    tf = _tile(w_down.shape[1], 512)

    cos_t, sin_t = _rope_tables(seq)
    x2d = x.reshape(batch * seq, d)
    mem2d = mem.reshape(batch * n_mem, d)
    row = lambda a: a.reshape(1, -1)

    for l in range(depth):
        w_in_l = w_in[l].astype(BF16)
        sgu_b_full = jnp.repeat(sgu_b[l].T, HEAD_DIM, axis=1)
        mk, mv = _memkv(mem2d, row(mem_norm[l]), w_mem_kv[l].astype(BF16), row(mk_norm[l]), n_mem)
        y_sgu, q, k, v, y_mem = _inproj(
            x2d, row(norm_mix[l]), w_in_l, cos_t, sin_t, row(sgu_norm[l]), sgu_w[l].astype(BF16),
            sgu_b_full, row(q_norm[l]), row(k_norm[l]), row(mq_norm[l]), mk, mv, seq, n_mem, tm)
        y_attn = _flash(q, k, v, batch, seq, tq, tk)
        x2d = _merge(x2d, row(norm_mix[l]), y_sgu, y_attn, y_mem, w_in_l, gate_b[l],
                     w_br_sgu[l].astype(BF16), w_br_attn[l].astype(BF16), w_br_mem[l].astype(BF16),
                     w_out[l].astype(BF16), tm, tn)
        x2d = _ffn(x2d, row(norm_ffn[l]), w_gate_up[l].astype(BF16), w_down[l].astype(BF16), tm, tf)
    return x2d.reshape(batch, seq, d)
```

```python
import functools
import math

import jax
import jax.numpy as jnp
from jax import lax
from jax.experimental import pallas as pl
from jax.experimental.pallas import tpu as pltpu

HEAD_DIM = 128
N_Q_HEADS = 8
N_KV_HEADS = 2
GQA_GROUP = N_Q_HEADS // N_KV_HEADS
N_MEM_HEADS = 4
SGU_GROUPS = 4
SGU_CHUNK = 128
GRID_W = 64
ROPE_THETA = 10000.0
N_BRANCH = 3
EPS = 1e-6

SGU_WIDTH = SGU_GROUPS * HEAD_DIM
Q_WIDTH = N_Q_HEADS * HEAD_DIM
KV_WIDTH = N_KV_HEADS * HEAD_DIM
MEM_WIDTH = N_MEM_HEADS * HEAD_DIM
MIX_WIDTH = 2 * SGU_WIDTH + Q_WIDTH + 2 * KV_WIDTH + MEM_WIDTH
OFF_Q = 2 * SGU_WIDTH
OFF_K = OFF_Q + Q_WIDTH
OFF_V = OFF_K + KV_WIDTH
OFF_MQ = OFF_V + KV_WIDTH

LOG2E = math.log2(math.e)
ATTN_SCALE = HEAD_DIM ** -0.5

V7X_VMEM_LIMIT_BYTES = 56 * 1024 * 1024
SUBLANES = 8

STAB_SLACK = 60.0
ROW_SUM_FLOOR = 2.0 ** -80
KEY_NORM_MARGIN = 1.01

BF16 = jnp.bfloat16
F32 = jnp.float32


def _dot(a, b):
    return jnp.dot(a, b, preferred_element_type=F32)


def _dot_nt(a, b):
    return lax.dot_general(a, b, (((1,), (1,)), ((), ())), preferred_element_type=F32)


def _rms(z, gain):
    return z * lax.rsqrt(jnp.mean(z * z, axis=-1, keepdims=True) + EPS) * gain


def _gelu(z):
    return 0.5 * z * (1.0 + lax.erf(z * math.sqrt(0.5)))


def _rope(z, cos, sin_signed, first_of_pair):
    partner = jnp.where(first_of_pair, pltpu.roll(z, 96, 1), pltpu.roll(z, 32, 1))
    return z * cos + partner * sin_signed


def _params(sem):
    return pltpu.CompilerParams(dimension_semantics=sem, vmem_limit_bytes=V7X_VMEM_LIMIT_BYTES)


def _memkv_kernel(mem_ref, mnorm_ref, w_ref, mkn_ref, mk_ref, mv_ref):
    hm = _rms(mem_ref[...], mnorm_ref[...]).astype(BF16)
    kv = _dot(hm, w_ref[...])
    for hh in range(N_MEM_HEADS):
        sl = slice(hh * HEAD_DIM, (hh + 1) * HEAD_DIM)
        mk_ref[:, sl] = _rms(kv[:, sl], mkn_ref[...]).astype(BF16)
    mv_ref[...] = kv[:, MEM_WIDTH:].astype(BF16)


def _memkv(mem2d, mem_norm, w_mem_kv, mk_norm, n_mem):
    rows, d = mem2d.shape
    return pl.pallas_call(
        _memkv_kernel,
        grid=(rows // n_mem,),
        in_specs=[
            pl.BlockSpec((n_mem, d), lambda i: (i, 0)),
            pl.BlockSpec((1, d), lambda i: (0, 0)),
            pl.BlockSpec((d, 2 * MEM_WIDTH), lambda i: (0, 0)),
            pl.BlockSpec((1, HEAD_DIM), lambda i: (0, 0)),
        ],
        out_specs=[
            pl.BlockSpec((n_mem, MEM_WIDTH), lambda i: (i, 0)),
            pl.BlockSpec((n_mem, MEM_WIDTH), lambda i: (i, 0)),
        ],
        out_shape=[jax.ShapeDtypeStruct((rows, MEM_WIDTH), BF16)] * 2,
        compiler_params=_params(("arbitrary",)),
        name="memkv",
    )(mem2d, mem_norm, w_mem_kv, mk_norm)


def _inproj_kernel(x_ref, nmix_ref, w_ref, cos_ref, sin_ref, sgun_ref, sguw_ref, sgub_ref,
                   qn_ref, kn_ref, mqn_ref, mk_ref, mv_ref,
                   ysgu_ref, q_ref, k_ref, v_ref, ymem_ref, kn2_ref):
    tm = x_ref.shape[0]
    h = _rms(x_ref[...], nmix_ref[...]).astype(BF16)

    zs = _gelu(_dot(h, w_ref[:, 0:OFF_Q]))
    for g in range(SGU_GROUPS):
        sl = slice(g * HEAD_DIM, (g + 1) * HEAD_DIM)
        u_g = zs[:, sl]
        vs_g = _rms(zs[:, SGU_WIDTH + g * HEAD_DIM:SGU_WIDTH + (g + 1) * HEAD_DIM],
                    sgun_ref[...]).astype(BF16)
        w_g = sguw_ref[g]
        for c in range(tm // SGU_CHUNK):
            rows = slice(c * SGU_CHUNK, (c + 1) * SGU_CHUNK)
            sp = _dot(w_g, vs_g[rows, :]) + sgub_ref[:, sl]
            ysgu_ref[rows, sl] = (u_g[rows, :] * sp).astype(BF16)

    cos = cos_ref[...]
    sin_signed = sin_ref[...]
    lane = lax.broadcasted_iota(jnp.int32, (tm, HEAD_DIM), 1)
    first_of_pair = (lane & 32) == 0
    zq = _dot(h, w_ref[:, OFF_Q:OFF_K])
    for hh in range(N_Q_HEADS):
        sl = slice(hh * HEAD_DIM, (hh + 1) * HEAD_DIM)
        qh = _rope(_rms(zq[:, sl], qn_ref[...]), cos, sin_signed, first_of_pair)
        q_ref[:, sl] = (qh * (ATTN_SCALE * LOG2E)).astype(BF16)
    zkv = _dot(h, w_ref[:, OFF_K:OFF_MQ])
    for hh in range(N_KV_HEADS):
        sl = slice(hh * HEAD_DIM, (hh + 1) * HEAD_DIM)
        kh = _rope(_rms(zkv[:, sl], kn_ref[...]), cos, sin_signed, first_of_pair)
        k_ref[:, sl] = kh.astype(BF16)
        kn2 = jnp.max(jnp.sum(kh * kh, axis=-1, keepdims=True), axis=0, keepdims=True)
        kn2_ref[:, sl] = jnp.broadcast_to(kn2, (kn2_ref.shape[0], HEAD_DIM))
        v_ref[:, 2 * hh * HEAD_DIM:(2 * hh + 1) * HEAD_DIM] = zkv[:, KV_WIDTH + hh * HEAD_DIM:
                                                                   KV_WIDTH + (hh + 1) * HEAD_DIM].astype(BF16)
        v_ref[:, (2 * hh + 1) * HEAD_DIM:(2 * hh + 2) * HEAD_DIM] = jnp.ones((tm, HEAD_DIM), BF16)

    zmq = _dot(h, w_ref[:, OFF_MQ:MIX_WIDTH])
    for hh in range(N_MEM_HEADS):
        sl = slice(hh * HEAD_DIM, (hh + 1) * HEAD_DIM)
        mq = (_rms(zmq[:, sl], mqn_ref[...]) * (ATTN_SCALE * LOG2E)).astype(BF16)
        s = _dot_nt(mq, mk_ref[:, sl])
        p = jnp.exp2(s - jnp.max(s, axis=-1, keepdims=True))
        o = _dot(p.astype(BF16), mv_ref[:, sl])
        ymem_ref[:, sl] = (o / jnp.sum(p, axis=-1, keepdims=True)).astype(BF16)


def _inproj(x2d, norm_mix, w_in, cos_t, sin_t, sgu_norm, sgu_w, sgu_b_full, q_norm, k_norm,
            mq_norm, mk, mv, seq, n_mem, tm):
    t, d = x2d.shape
    tiles_per_seq = seq // tm
    const = lambda i: (0, 0)
    row = lambda i: (i, 0)
    return pl.pallas_call(
        _inproj_kernel,
        grid=(t // tm,),
        in_specs=[
            pl.BlockSpec((tm, d), row),
            pl.BlockSpec((1, d), const),
            pl.BlockSpec((d, MIX_WIDTH), const),
            pl.BlockSpec((tm, HEAD_DIM), lambda i: (i % tiles_per_seq, 0)),
            pl.BlockSpec((tm, HEAD_DIM), lambda i: (i % tiles_per_seq, 0)),
            pl.BlockSpec((1, HEAD_DIM), const),
            pl.BlockSpec((SGU_GROUPS, SGU_CHUNK, SGU_CHUNK), lambda i: (0, 0, 0)),
            pl.BlockSpec((SGU_CHUNK, SGU_WIDTH), const),
            pl.BlockSpec((1, HEAD_DIM), const),
            pl.BlockSpec((1, HEAD_DIM), const),
            pl.BlockSpec((1, HEAD_DIM), const),
            pl.BlockSpec((n_mem, MEM_WIDTH), lambda i: (i // tiles_per_seq, 0)),
            pl.BlockSpec((n_mem, MEM_WIDTH), lambda i: (i // tiles_per_seq, 0)),
        ],
        out_specs=[
            pl.BlockSpec((tm, SGU_WIDTH), row),
            pl.BlockSpec((tm, Q_WIDTH), row),
            pl.BlockSpec((tm, KV_WIDTH), row),
            pl.BlockSpec((tm, 2 * KV_WIDTH), row),
            pl.BlockSpec((tm, MEM_WIDTH), row),
            pl.BlockSpec((SUBLANES, KV_WIDTH), row),
        ],
        out_shape=[
            jax.ShapeDtypeStruct((t, SGU_WIDTH), BF16),
            jax.ShapeDtypeStruct((t, Q_WIDTH), BF16),
            jax.ShapeDtypeStruct((t, KV_WIDTH), BF16),
            jax.ShapeDtypeStruct((t, 2 * KV_WIDTH), BF16),
            jax.ShapeDtypeStruct((t, MEM_WIDTH), BF16),
            jax.ShapeDtypeStruct((t // tm * SUBLANES, KV_WIDTH), F32),
        ],
        compiler_params=_params(("arbitrary",)),
        name="inproj",
    )(x2d, norm_mix, w_in, cos_t, sin_t, sgu_norm, sgu_w, sgu_b_full, q_norm, k_norm, mq_norm, mk, mv)


def _flash_kernel(q_ref, k_ref, v_ref, o_ref, m_sc, l_sc, acc_sc):
    ki = pl.program_id(3)

    @pl.when(ki == 0)
    def _():
        m_sc[...] = jnp.full(m_sc.shape, -jnp.inf, F32)
        l_sc[...] = jnp.zeros(l_sc.shape, F32)
        acc_sc[...] = jnp.zeros(acc_sc.shape, F32)

    k = k_ref[...]
    v = v_ref[...]
    for hh in range(GQA_GROUP):
        sl = slice(hh * HEAD_DIM, (hh + 1) * HEAD_DIM)
        s = _dot_nt(q_ref[:, sl], k)
        m_prev = m_sc[hh]
        m_new = jnp.maximum(m_prev, jnp.max(s, axis=-1, keepdims=True))
        alpha = jnp.exp2(m_prev - m_new)
        p = jnp.exp2(s - m_new[:, 0:1])
        l_sc[hh] = alpha * l_sc[hh] + jnp.sum(p, axis=-1, keepdims=True)
        acc_sc[hh] = alpha * acc_sc[hh] + _dot(p.astype(BF16), v)
        m_sc[hh] = m_new

    @pl.when(ki == pl.num_programs(3) - 1)
    def _():
        for hh in range(GQA_GROUP):
            sl = slice(hh * HEAD_DIM, (hh + 1) * HEAD_DIM)
            o_ref[:, sl] = (acc_sc[hh] / l_sc[hh]).astype(BF16)


def _flash_fast_kernel(kmax_ref, q_ref, k_ref, v_ref, o_ref, lmin_ref, c_sc, acc_sc):
    ki = pl.program_id(3)
    tq = q_ref.shape[0]
    tk = k_ref.shape[0]

    @pl.when(ki == 0)
    def _():
        kmax = kmax_ref[pl.program_id(0), pl.program_id(1)]
        for hh in range(GQA_GROUP):
            qf = q_ref[:, hh * HEAD_DIM:(hh + 1) * HEAD_DIM].astype(F32)
            qn = jnp.sqrt(jnp.sum(qf * qf, axis=-1, keepdims=True))
            c_sc[hh] = jnp.broadcast_to(qn * kmax - STAB_SLACK, (tq, HEAD_DIM))
        acc_sc[...] = jnp.zeros(acc_sc.shape, F32)

    k = k_ref[...]
    v = v_ref[...]
    for hh in range(GQA_GROUP):
        s = _dot_nt(q_ref[:, hh * HEAD_DIM:(hh + 1) * HEAD_DIM], k)
        p = jnp.exp2(s - jnp.tile(c_sc[hh], (1, tk // HEAD_DIM)))
        acc_sc[hh] += _dot(p.astype(BF16), v)

    @pl.when(ki == pl.num_programs(3) - 1)
    def _():
        lmin = None
        for hh in range(GQA_GROUP):
            acc = acc_sc[hh]
            l = acc[:, HEAD_DIM:]
            o_ref[:, hh * HEAD_DIM:(hh + 1) * HEAD_DIM] = (acc[:, :HEAD_DIM] / l).astype(BF16)
            lm = jnp.min(l, axis=0, keepdims=True)
            lmin = lm if lmin is None else jnp.minimum(lmin, lm)
        lmin_ref[...] = jnp.broadcast_to(lmin, lmin_ref.shape)


def _flash_fast(kmax, q, k, v_ext, batch, seq, tq, tk):
    t = q.shape[0]
    nq = seq // tq
    nk = seq // tk
    group_w = GQA_GROUP * HEAD_DIM
    return pl.pallas_call(
        _flash_fast_kernel,
        grid=(batch, N_KV_HEADS, nq, nk),
        in_specs=[
            pl.BlockSpec(memory_space=pltpu.SMEM),
            pl.BlockSpec((tq, group_w), lambda b, g, qi, ki: (b * nq + qi, g)),
            pl.BlockSpec((tk, HEAD_DIM), lambda b, g, qi, ki: (b * nk + ki, g)),
            pl.BlockSpec((tk, 2 * HEAD_DIM), lambda b, g, qi, ki: (b * nk + ki, g)),
        ],
        out_specs=[
            pl.BlockSpec((tq, group_w), lambda b, g, qi, ki: (b * nq + qi, g)),
            pl.BlockSpec((SUBLANES, HEAD_DIM), lambda b, g, qi, ki: (b * nq + qi, g)),
        ],
        out_shape=[
            jax.ShapeDtypeStruct((t, Q_WIDTH), BF16),
            jax.ShapeDtypeStruct((batch * nq * SUBLANES, N_KV_HEADS * HEAD_DIM), F32),
        ],
        scratch_shapes=[
            pltpu.VMEM((GQA_GROUP, tq, HEAD_DIM), F32),
            pltpu.VMEM((GQA_GROUP, tq, 2 * HEAD_DIM), F32),
        ],
        compiler_params=_params(("arbitrary", "arbitrary", "arbitrary", "arbitrary")),
        name="flash_fast",
    )(kmax, q, k, v_ext)


def _flash(q, k, v_ext, batch, seq, tq, tk):
    t = q.shape[0]
    nq = seq // tq
    nk = seq // tk
    group_w = GQA_GROUP * HEAD_DIM
    return pl.pallas_call(
        _flash_kernel,
        grid=(batch, N_KV_HEADS, nq, nk),
        in_specs=[
            pl.BlockSpec((tq, group_w), lambda b, g, qi, ki: (b * nq + qi, g)),
            pl.BlockSpec((tk, HEAD_DIM), lambda b, g, qi, ki: (b * nk + ki, g)),
            pl.BlockSpec((tk, HEAD_DIM), lambda b, g, qi, ki: (b * nk + ki, 2 * g)),
        ],
        out_specs=pl.BlockSpec((tq, group_w), lambda b, g, qi, ki: (b * nq + qi, g)),
        out_shape=jax.ShapeDtypeStruct((t, Q_WIDTH), BF16),
        scratch_shapes=[
            pltpu.VMEM((GQA_GROUP, tq, HEAD_DIM), F32),
            pltpu.VMEM((GQA_GROUP, tq, HEAD_DIM), F32),
            pltpu.VMEM((GQA_GROUP, tq, HEAD_DIM), F32),
        ],
        compiler_params=_params(("arbitrary", "arbitrary", "arbitrary", "arbitrary")),
        name="flash",
    )(q, k, v_ext)


def _attention(q, k, v_ext, kn2, batch, seq, tm, tq, tk, tq_fast, tk_fast):
    kmax = jnp.sqrt(jnp.max(
        kn2.reshape(batch, seq // tm, SUBLANES, N_KV_HEADS, HEAD_DIM)[:, :, 0, :, 0], axis=1))
    y_fast, lmin = _flash_fast(kmax * KEY_NORM_MARGIN, q, k, v_ext, batch, seq, tq_fast, tk_fast)
    trusted = jnp.all(lmin >= ROW_SUM_FLOOR)
    return lax.cond(trusted, lambda: y_fast, lambda: _flash(q, k, v_ext, batch, seq, tq, tk))


def _merge_kernel(x_ref, nmix_ref, ysgu_ref, yattn_ref, ymem_ref, wg0_ref, wg1_ref, wg2_ref,
                  gb_ref, wbs_ref, wba_ref, wbm_ref, wout_ref, o_ref, h_sc):
    j = pl.program_id(1)

    @pl.when(j == 0)
    def _():
        x = x_ref[...]
        h_sc[...] = _rms(x, nmix_ref[...]).astype(BF16)
        o_ref[...] = x

    h = h_sc[...]
    y = jax.nn.sigmoid(_dot(h, wg0_ref[...]) + gb_ref[0:1, :]) * _dot(ysgu_ref[...], wbs_ref[...])
    y += jax.nn.sigmoid(_dot(h, wg1_ref[...]) + gb_ref[1:2, :]) * _dot(yattn_ref[...], wba_ref[...])
    y += jax.nn.sigmoid(_dot(h, wg2_ref[...]) + gb_ref[2:3, :]) * _dot(ymem_ref[...], wbm_ref[...])
    o_ref[...] += _dot(y.astype(BF16), wout_ref[...])


def _merge(x2d, norm_mix, y_sgu, y_attn, y_mem, w_in, gate_b, w_br_sgu, w_br_attn, w_br_mem,
           w_out, tm, tn):
    t, d = x2d.shape
    nj = d // tn
    gate0 = MIX_WIDTH // tn
    row = lambda i, j: (i, 0)
    col = lambda i, j: (0, j)
    return pl.pallas_call(
        _merge_kernel,
        grid=(t // tm, nj),
        in_specs=[
            pl.BlockSpec((tm, d), row),
            pl.BlockSpec((1, d), lambda i, j: (0, 0)),
            pl.BlockSpec((tm, SGU_WIDTH), row),
            pl.BlockSpec((tm, Q_WIDTH), row),
            pl.BlockSpec((tm, MEM_WIDTH), row),
            pl.BlockSpec((d, tn), lambda i, j: (0, gate0 + j)),
            pl.BlockSpec((d, tn), lambda i, j: (0, gate0 + nj + j)),
            pl.BlockSpec((d, tn), lambda i, j: (0, gate0 + 2 * nj + j)),
            pl.BlockSpec((N_BRANCH, tn), col),
            pl.BlockSpec((SGU_WIDTH, tn), col),
            pl.BlockSpec((Q_WIDTH, tn), col),
            pl.BlockSpec((MEM_WIDTH, tn), col),
            pl.BlockSpec((tn, d), lambda i, j: (j, 0)),
        ],
        out_specs=pl.BlockSpec((tm, d), row),
        out_shape=jax.ShapeDtypeStruct((t, d), F32),
        scratch_shapes=[pltpu.VMEM((tm, d), BF16)],
        compiler_params=_params(("arbitrary", "arbitrary")),
        name="merge",
    )(x2d, norm_mix, y_sgu, y_attn, y_mem, w_in, w_in, w_in, gate_b, w_br_sgu, w_br_attn,
      w_br_mem, w_out)


def _ffn_kernel(x_ref, nffn_ref, wg_ref, wu_ref, wd_ref, o_ref, h_sc):
    j = pl.program_id(1)

    @pl.when(j == 0)
    def _():
        x = x_ref[...]
        h_sc[...] = _rms(x, nffn_ref[...]).astype(BF16)
        o_ref[...] = x

    h = h_sc[...]
    a = _dot(h, wg_ref[...])
    b = _dot(h, wu_ref[...])
    o_ref[...] += _dot((a * jax.nn.sigmoid(a) * b).astype(BF16), wd_ref[...])


def _ffn(x2d, norm_ffn, w_gate_up, w_down, tm, tf):
    t, d = x2d.shape
    d_ff = w_down.shape[0]
    nj = d_ff // tf
    row = lambda i, j: (i, 0)
    return pl.pallas_call(
        _ffn_kernel,
        grid=(t // tm, nj),
        in_specs=[
            pl.BlockSpec((tm, d), row),
            pl.BlockSpec((1, d), lambda i, j: (0, 0)),
            pl.BlockSpec((d, tf), lambda i, j: (0, j)),
            pl.BlockSpec((d, tf), lambda i, j: (0, nj + j)),
            pl.BlockSpec((tf, d), lambda i, j: (j, 0)),
        ],
        out_specs=pl.BlockSpec((tm, d), row),
        out_shape=jax.ShapeDtypeStruct((t, d), F32),
        scratch_shapes=[pltpu.VMEM((tm, d), BF16)],
        compiler_params=_params(("arbitrary", "arbitrary")),
        name="ffn",
    )(x2d, norm_ffn, w_gate_up, w_gate_up, w_down)


def _rope_tables(seq):
    quarter = HEAD_DIM // 4
    inv = ROPE_THETA ** (-jnp.arange(0, 2 * quarter, 2, dtype=F32) / (2 * quarter))
    pos = jnp.arange(seq, dtype=jnp.int32)
    ar = (pos // GRID_W).astype(F32)[:, None] * inv[None, :]
    ac = (pos % GRID_W).astype(F32)[:, None] * inv[None, :]
    cos_t = jnp.concatenate([jnp.cos(ar), jnp.cos(ar), jnp.cos(ac), jnp.cos(ac)], axis=-1)
    sin_t = jnp.concatenate([-jnp.sin(ar), jnp.sin(ar), -jnp.sin(ac), jnp.sin(ac)], axis=-1)
    return cos_t, sin_t


def _tile(n, want):
    if n <= want:
        return n
    for cand in range(want, 0, -128):
        if n % cand == 0:
            return cand
    return n


def kernel(x, mem, norm_mix, w_in, gate_b, sgu_norm, sgu_w, sgu_b, q_norm, k_norm, mem_norm,
           w_mem_kv, mq_norm, mk_norm, w_br_sgu, w_br_attn, w_br_mem, w_out, norm_ffn,
           w_gate_up, w_down):
    batch, seq, d = x.shape
    n_mem = mem.shape[1]
    depth = w_in.shape[0]
    assert seq % SGU_CHUNK == 0 and seq % GRID_W == 0

    tm = _tile(seq, 512)
    tq = _tile(seq, 512)
    tk = _tile(seq, 1024)
    tq_fast = _tile(seq, 1024)
    tk_fast = _tile(seq, 1024)
    tn = _tile(d, 512)
    tf = _tile(w_down.shape[1], 512)

    cos_t, sin_t = _rope_tables(seq)
    x2d = x.reshape(batch * seq, d)
    mem2d = mem.reshape(batch * n_mem, d)
    row = lambda a: a.reshape(1, -1)

    for l in range(depth):
        w_in_l = w_in[l].astype(BF16)
        sgu_b_full = jnp.repeat(sgu_b[l].T, HEAD_DIM, axis=1)
        mk, mv = _memkv(mem2d, row(mem_norm[l]), w_mem_kv[l].astype(BF16), row(mk_norm[l]), n_mem)
        y_sgu, q, k, v_ext, y_mem, kn2 = _inproj(
            x2d, row(norm_mix[l]), w_in_l, cos_t, sin_t, row(sgu_norm[l]), sgu_w[l].astype(BF16),
            sgu_b_full, row(q_norm[l]), row(k_norm[l]), row(mq_norm[l]), mk, mv, seq, n_mem, tm)
        y_attn = _attention(q, k, v_ext, kn2, batch, seq, tm, tq, tk, tq_fast, tk_fast)
        x2d = _merge(x2d, row(norm_mix[l]), y_sgu, y_attn, y_mem, w_in_l, gate_b[l],
                     w_br_sgu[l].astype(BF16), w_br_attn[l].astype(BF16), w_br_mem[l].astype(BF16),
                     w_out[l].astype(BF16), tm, tn)
        x2d = _ffn(x2d, row(norm_ffn[l]), w_gate_up[l].astype(BF16), w_down[l].astype(BF16), tm, tf)
    return x2d.reshape(batch, seq, d)
```

```python
import functools
import math

import jax
import jax.numpy as jnp
from jax import lax
from jax.experimental import pallas as pl
from jax.experimental.pallas import tpu as pltpu

HEAD_DIM = 128
N_Q_HEADS = 8
N_KV_HEADS = 2
GQA_GROUP = N_Q_HEADS // N_KV_HEADS
N_MEM_HEADS = 4
SGU_GROUPS = 4
SGU_CHUNK = 128
GRID_W = 64
ROPE_THETA = 10000.0
N_BRANCH = 3
EPS = 1e-6

SGU_WIDTH = SGU_GROUPS * HEAD_DIM
Q_WIDTH = N_Q_HEADS * HEAD_DIM
KV_WIDTH = N_KV_HEADS * HEAD_DIM
MEM_WIDTH = N_MEM_HEADS * HEAD_DIM
MIX_WIDTH = 2 * SGU_WIDTH + Q_WIDTH + 2 * KV_WIDTH + MEM_WIDTH
OFF_Q = 2 * SGU_WIDTH
OFF_K = OFF_Q + Q_WIDTH
OFF_V = OFF_K + KV_WIDTH
OFF_MQ = OFF_V + KV_WIDTH

LOG2E = math.log2(math.e)
ATTN_SCALE = HEAD_DIM ** -0.5

V7X_VMEM_LIMIT_BYTES = 56 * 1024 * 1024
SUBLANES = 8

STAB_SLACK = 60.0
ROW_SUM_FLOOR = 2.0 ** -80
KEY_NORM_MARGIN = 1.01

BF16 = jnp.bfloat16
F32 = jnp.float32


def _dot(a, b):
    return jnp.dot(a, b, preferred_element_type=F32)


def _dot_nt(a, b):
    return lax.dot_general(a, b, (((1,), (1,)), ((), ())), preferred_element_type=F32)


def _rms(z, gain):
    return z * lax.rsqrt(jnp.mean(z * z, axis=-1, keepdims=True) + EPS) * gain


def _gelu(z):
    return 0.5 * z * (1.0 + lax.erf(z * math.sqrt(0.5)))


def _rope(z, cos, sin_signed, first_of_pair):
    partner = jnp.where(first_of_pair, pltpu.roll(z, 96, 1), pltpu.roll(z, 32, 1))
    return z * cos + partner * sin_signed


def _params(sem):
    return pltpu.CompilerParams(dimension_semantics=sem, vmem_limit_bytes=V7X_VMEM_LIMIT_BYTES)


def _memkv_kernel(mem_ref, mnorm_ref, w_ref, mkn_ref, mk_ref, mv_ref):
    hm = _rms(mem_ref[...], mnorm_ref[...]).astype(BF16)
    kv = _dot(hm, w_ref[...])
    for hh in range(N_MEM_HEADS):
        sl = slice(hh * HEAD_DIM, (hh + 1) * HEAD_DIM)
        mk_ref[:, sl] = _rms(kv[:, sl], mkn_ref[...]).astype(BF16)
    mv_ref[...] = kv[:, MEM_WIDTH:].astype(BF16)


def _memkv(mem2d, mem_norm, w_mem_kv, mk_norm, n_mem):
    rows, d = mem2d.shape
    return pl.pallas_call(
        _memkv_kernel,
        grid=(rows // n_mem,),
        in_specs=[
            pl.BlockSpec((n_mem, d), lambda i: (i, 0)),
            pl.BlockSpec((1, d), lambda i: (0, 0)),
            pl.BlockSpec((d, 2 * MEM_WIDTH), lambda i: (0, 0)),
            pl.BlockSpec((1, HEAD_DIM), lambda i: (0, 0)),
        ],
        out_specs=[
            pl.BlockSpec((n_mem, MEM_WIDTH), lambda i: (i, 0)),
            pl.BlockSpec((n_mem, MEM_WIDTH), lambda i: (i, 0)),
        ],
        out_shape=[jax.ShapeDtypeStruct((rows, MEM_WIDTH), BF16)] * 2,
        compiler_params=_params(("arbitrary",)),
        name="memkv",
    )(mem2d, mem_norm, w_mem_kv, mk_norm)


def _inproj_kernel(x_ref, nmix_ref, w_ref, cos_ref, sin_ref, sgun_ref, sguw_ref, sgub_ref,
                   qn_ref, kn_ref, mqn_ref, mk_ref, mv_ref,
                   ysgu_ref, q_ref, k_ref, v_ref, ymem_ref, kn2_ref):
    tm = x_ref.shape[0]
    h = _rms(x_ref[...], nmix_ref[...]).astype(BF16)

    zs = _gelu(_dot(h, w_ref[:, 0:OFF_Q]))
    for g in range(SGU_GROUPS):
        sl = slice(g * HEAD_DIM, (g + 1) * HEAD_DIM)
        u_g = zs[:, sl]
        vs_g = _rms(zs[:, SGU_WIDTH + g * HEAD_DIM:SGU_WIDTH + (g + 1) * HEAD_DIM],
                    sgun_ref[...]).astype(BF16)
        w_g = sguw_ref[g]
        for c in range(tm // SGU_CHUNK):
            rows = slice(c * SGU_CHUNK, (c + 1) * SGU_CHUNK)
            sp = _dot(w_g, vs_g[rows, :]) + sgub_ref[:, sl]
            ysgu_ref[rows, sl] = (u_g[rows, :] * sp).astype(BF16)

    cos = cos_ref[...]
    sin_signed = sin_ref[...]
    lane = lax.broadcasted_iota(jnp.int32, (tm, HEAD_DIM), 1)
    first_of_pair = (lane & 32) == 0
    zq = _dot(h, w_ref[:, OFF_Q:OFF_K])
    for hh in range(N_Q_HEADS):
        sl = slice(hh * HEAD_DIM, (hh + 1) * HEAD_DIM)
        qh = _rope(_rms(zq[:, sl], qn_ref[...]), cos, sin_signed, first_of_pair)
        q_ref[:, sl] = (qh * (ATTN_SCALE * LOG2E)).astype(BF16)
    zkv = _dot(h, w_ref[:, OFF_K:OFF_MQ])
    for hh in range(N_KV_HEADS):
        sl = slice(hh * HEAD_DIM, (hh + 1) * HEAD_DIM)
        kh = _rope(_rms(zkv[:, sl], kn_ref[...]), cos, sin_signed, first_of_pair)
        k_ref[:, sl] = kh.astype(BF16)
        kn2 = jnp.max(jnp.sum(kh * kh, axis=-1, keepdims=True), axis=0, keepdims=True)
        kn2_ref[:, sl] = jnp.broadcast_to(kn2, (kn2_ref.shape[0], HEAD_DIM))
        v_ref[:, 2 * hh * HEAD_DIM:(2 * hh + 1) * HEAD_DIM] = zkv[:, KV_WIDTH + hh * HEAD_DIM:
                                                                   KV_WIDTH + (hh + 1) * HEAD_DIM].astype(BF16)
        v_ref[:, (2 * hh + 1) * HEAD_DIM:(2 * hh + 2) * HEAD_DIM] = jnp.ones((tm, HEAD_DIM), BF16)

    zmq = _dot(h, w_ref[:, OFF_MQ:MIX_WIDTH])
    for hh in range(N_MEM_HEADS):
        sl = slice(hh * HEAD_DIM, (hh + 1) * HEAD_DIM)
        mq = (_rms(zmq[:, sl], mqn_ref[...]) * (ATTN_SCALE * LOG2E)).astype(BF16)
        s = _dot_nt(mq, mk_ref[:, sl])
        p = jnp.exp2(s - jnp.max(s, axis=-1, keepdims=True))
        o = _dot(p.astype(BF16), mv_ref[:, sl])
        ymem_ref[:, sl] = (o / jnp.sum(p, axis=-1, keepdims=True)).astype(BF16)


def _inproj(x2d, norm_mix, w_in, cos_t, sin_t, sgu_norm, sgu_w, sgu_b_full, q_norm, k_norm,
            mq_norm, mk, mv, seq, n_mem, tm):
    t, d = x2d.shape
    tiles_per_seq = seq // tm
    const = lambda i: (0, 0)
    row = lambda i: (i, 0)
    return pl.pallas_call(
        _inproj_kernel,
        grid=(t // tm,),
        in_specs=[
            pl.BlockSpec((tm, d), row),
            pl.BlockSpec((1, d), const),
            pl.BlockSpec((d, MIX_WIDTH), const),
            pl.BlockSpec((tm, HEAD_DIM), lambda i: (i % tiles_per_seq, 0)),
            pl.BlockSpec((tm, HEAD_DIM), lambda i: (i % tiles_per_seq, 0)),
            pl.BlockSpec((1, HEAD_DIM), const),
            pl.BlockSpec((SGU_GROUPS, SGU_CHUNK, SGU_CHUNK), lambda i: (0, 0, 0)),
            pl.BlockSpec((SGU_CHUNK, SGU_WIDTH), const),
            pl.BlockSpec((1, HEAD_DIM), const),
            pl.BlockSpec((1, HEAD_DIM), const),
            pl.BlockSpec((1, HEAD_DIM), const),
            pl.BlockSpec((n_mem, MEM_WIDTH), lambda i: (i // tiles_per_seq, 0)),
            pl.BlockSpec((n_mem, MEM_WIDTH), lambda i: (i // tiles_per_seq, 0)),
        ],
        out_specs=[
            pl.BlockSpec((tm, SGU_WIDTH), row),
            pl.BlockSpec((tm, Q_WIDTH), row),
            pl.BlockSpec((tm, KV_WIDTH), row),
            pl.BlockSpec((tm, 2 * KV_WIDTH), row),
            pl.BlockSpec((tm, MEM_WIDTH), row),
            pl.BlockSpec((SUBLANES, KV_WIDTH), row),
        ],
        out_shape=[
            jax.ShapeDtypeStruct((t, SGU_WIDTH), BF16),
            jax.ShapeDtypeStruct((t, Q_WIDTH), BF16),
            jax.ShapeDtypeStruct((t, KV_WIDTH), BF16),
            jax.ShapeDtypeStruct((t, 2 * KV_WIDTH), BF16),
            jax.ShapeDtypeStruct((t, MEM_WIDTH), BF16),
            jax.ShapeDtypeStruct((t // tm * SUBLANES, KV_WIDTH), F32),
        ],
        compiler_params=_params(("arbitrary",)),
        name="inproj",
    )(x2d, norm_mix, w_in, cos_t, sin_t, sgu_norm, sgu_w, sgu_b_full, q_norm, k_norm, mq_norm, mk, mv)


def _flash_kernel(q_ref, k_ref, v_ref, o_ref, m_sc, l_sc, acc_sc):
    ki = pl.program_id(3)

    @pl.when(ki == 0)
    def _():
        m_sc[...] = jnp.full(m_sc.shape, -jnp.inf, F32)
        l_sc[...] = jnp.zeros(l_sc.shape, F32)
        acc_sc[...] = jnp.zeros(acc_sc.shape, F32)

    k = k_ref[...]
    v = v_ref[...]
    for hh in range(GQA_GROUP):
        sl = slice(hh * HEAD_DIM, (hh + 1) * HEAD_DIM)
        s = _dot_nt(q_ref[:, sl], k)
        m_prev = m_sc[hh]
        m_new = jnp.maximum(m_prev, jnp.max(s, axis=-1, keepdims=True))
        alpha = jnp.exp2(m_prev - m_new)
        p = jnp.exp2(s - m_new[:, 0:1])
        l_sc[hh] = alpha * l_sc[hh] + jnp.sum(p, axis=-1, keepdims=True)
        acc_sc[hh] = alpha * acc_sc[hh] + _dot(p.astype(BF16), v)
        m_sc[hh] = m_new

    @pl.when(ki == pl.num_programs(3) - 1)
    def _():
        for hh in range(GQA_GROUP):
            sl = slice(hh * HEAD_DIM, (hh + 1) * HEAD_DIM)
            o_ref[:, sl] = (acc_sc[hh] / l_sc[hh]).astype(BF16)


def _flash_fast_kernel(kmax_ref, q_ref, k_ref, v_ref, o_ref, lmin_ref, c_sc, acc_sc):
    ki = pl.program_id(3)
    tq = q_ref.shape[0]
    tk = k_ref.shape[0]

    @pl.when(ki == 0)
    def _():
        kmax = kmax_ref[pl.program_id(0), pl.program_id(1)]
        for hh in range(GQA_GROUP):
            qf = q_ref[:, hh * HEAD_DIM:(hh + 1) * HEAD_DIM].astype(F32)
            qn = jnp.sqrt(jnp.sum(qf * qf, axis=-1, keepdims=True))
            c_sc[hh] = jnp.broadcast_to(qn * kmax - STAB_SLACK, (tq, HEAD_DIM))
        acc_sc[...] = jnp.zeros(acc_sc.shape, F32)

    k = k_ref[...]
    v = v_ref[...]
    for hh in range(GQA_GROUP):
        s = _dot_nt(q_ref[:, hh * HEAD_DIM:(hh + 1) * HEAD_DIM], k)
        p = jnp.exp2(s - jnp.tile(c_sc[hh], (1, tk // HEAD_DIM)))
        acc_sc[hh] += _dot(p.astype(BF16), v)

    @pl.when(ki == pl.num_programs(3) - 1)
    def _():
        lmin = None
        for hh in range(GQA_GROUP):
            acc = acc_sc[hh]
            l = acc[:, HEAD_DIM:]
            o_ref[:, hh * HEAD_DIM:(hh + 1) * HEAD_DIM] = (acc[:, :HEAD_DIM] / l).astype(BF16)
            lm = jnp.min(l, axis=0, keepdims=True)
            lmin = lm if lmin is None else jnp.minimum(lmin, lm)
        lmin_ref[...] = jnp.broadcast_to(lmin, lmin_ref.shape)


def _flash_fast(kmax, q, k, v_ext, batch, seq, tq, tk):
    t = q.shape[0]
    nq = seq // tq
    nk = seq // tk
    group_w = GQA_GROUP * HEAD_DIM
    return pl.pallas_call(
        _flash_fast_kernel,
        grid=(batch, N_KV_HEADS, nq, nk),
        in_specs=[
            pl.BlockSpec(memory_space=pltpu.SMEM),
            pl.BlockSpec((tq, group_w), lambda b, g, qi, ki: (b * nq + qi, g)),
            pl.BlockSpec((tk, HEAD_DIM), lambda b, g, qi, ki: (b * nk + ki, g)),
            pl.BlockSpec((tk, 2 * HEAD_DIM), lambda b, g, qi, ki: (b * nk + ki, g)),
        ],
        out_specs=[
            pl.BlockSpec((tq, group_w), lambda b, g, qi, ki: (b * nq + qi, g)),
            pl.BlockSpec((SUBLANES, HEAD_DIM), lambda b, g, qi, ki: (b * nq + qi, g)),
        ],
        out_shape=[
            jax.ShapeDtypeStruct((t, Q_WIDTH), BF16),
            jax.ShapeDtypeStruct((batch * nq * SUBLANES, N_KV_HEADS * HEAD_DIM), F32),
        ],
        scratch_shapes=[
            pltpu.VMEM((GQA_GROUP, tq, HEAD_DIM), F32),
            pltpu.VMEM((GQA_GROUP, tq, 2 * HEAD_DIM), F32),
        ],
        compiler_params=_params(("arbitrary", "arbitrary", "arbitrary", "arbitrary")),
        name="flash_fast",
    )(kmax, q, k, v_ext)


def _flash(q, k, v_ext, batch, seq, tq, tk):
    t = q.shape[0]
    nq = seq // tq
    nk = seq // tk
    group_w = GQA_GROUP * HEAD_DIM
    return pl.pallas_call(
        _flash_kernel,
        grid=(batch, N_KV_HEADS, nq, nk),
        in_specs=[
            pl.BlockSpec((tq, group_w), lambda b, g, qi, ki: (b * nq + qi, g)),
            pl.BlockSpec((tk, HEAD_DIM), lambda b, g, qi, ki: (b * nk + ki, g)),
            pl.BlockSpec((tk, HEAD_DIM), lambda b, g, qi, ki: (b * nk + ki, 2 * g)),
        ],
        out_specs=pl.BlockSpec((tq, group_w), lambda b, g, qi, ki: (b * nq + qi, g)),
        out_shape=jax.ShapeDtypeStruct((t, Q_WIDTH), BF16),
        scratch_shapes=[
            pltpu.VMEM((GQA_GROUP, tq, HEAD_DIM), F32),
            pltpu.VMEM((GQA_GROUP, tq, HEAD_DIM), F32),
            pltpu.VMEM((GQA_GROUP, tq, HEAD_DIM), F32),
        ],
        compiler_params=_params(("arbitrary", "arbitrary", "arbitrary", "arbitrary")),
        name="flash",
    )(q, k, v_ext)


def _attention(q, k, v_ext, kn2, batch, seq, tm, tq, tk, tq_fast, tk_fast):
    kmax = jnp.sqrt(jnp.max(
        kn2.reshape(batch, seq // tm, SUBLANES, N_KV_HEADS, HEAD_DIM)[:, :, 0, :, 0], axis=1))
    y_fast, lmin = _flash_fast(kmax * KEY_NORM_MARGIN, q, k, v_ext, batch, seq, tq_fast, tk_fast)
    trusted = jnp.all(lmin >= ROW_SUM_FLOOR)
    return lax.cond(trusted, lambda: y_fast, lambda: _flash(q, k, v_ext, batch, seq, tq, tk))


def _merge_kernel(x_ref, nmix_ref, ysgu_ref, yattn_ref, ymem_ref, wg0_ref, wg1_ref, wg2_ref,
                  gb_ref, wbs_ref, wba_ref, wbm_ref, wout_ref, o_ref, h_sc):
    j = pl.program_id(1)

    @pl.when(j == 0)
    def _():
        x = x_ref[...]
        h_sc[...] = _rms(x, nmix_ref[...]).astype(BF16)
        o_ref[...] = x

    h = h_sc[...]
    y = jax.nn.sigmoid(_dot(h, wg0_ref[...]) + gb_ref[0:1, :]) * _dot(ysgu_ref[...], wbs_ref[...])
    y += jax.nn.sigmoid(_dot(h, wg1_ref[...]) + gb_ref[1:2, :]) * _dot(yattn_ref[...], wba_ref[...])
    y += jax.nn.sigmoid(_dot(h, wg2_ref[...]) + gb_ref[2:3, :]) * _dot(ymem_ref[...], wbm_ref[...])
    o_ref[...] += _dot(y.astype(BF16), wout_ref[...])


def _merge(x2d, norm_mix, y_sgu, y_attn, y_mem, w_in, gate_b, w_br_sgu, w_br_attn, w_br_mem,
           w_out, tm, tn):
    t, d = x2d.shape
    nj = d // tn
    gate0 = MIX_WIDTH // tn
    row = lambda i, j: (i, 0)
    col = lambda i, j: (0, j)
    return pl.pallas_call(
        _merge_kernel,
        grid=(t // tm, nj),
        in_specs=[
            pl.BlockSpec((tm, d), row),
            pl.BlockSpec((1, d), lambda i, j: (0, 0)),
            pl.BlockSpec((tm, SGU_WIDTH), row),
            pl.BlockSpec((tm, Q_WIDTH), row),
            pl.BlockSpec((tm, MEM_WIDTH), row),
            pl.BlockSpec((d, tn), lambda i, j: (0, gate0 + j)),
            pl.BlockSpec((d, tn), lambda i, j: (0, gate0 + nj + j)),
            pl.BlockSpec((d, tn), lambda i, j: (0, gate0 + 2 * nj + j)),
            pl.BlockSpec((N_BRANCH, tn), col),
            pl.BlockSpec((SGU_WIDTH, tn), col),
            pl.BlockSpec((Q_WIDTH, tn), col),
            pl.BlockSpec((MEM_WIDTH, tn), col),
            pl.BlockSpec((tn, d), lambda i, j: (j, 0)),
        ],
        out_specs=pl.BlockSpec((tm, d), row),
        out_shape=jax.ShapeDtypeStruct((t, d), F32),
        scratch_shapes=[pltpu.VMEM((tm, d), BF16)],
        compiler_params=_params(("arbitrary", "arbitrary")),
        name="merge",
    )(x2d, norm_mix, y_sgu, y_attn, y_mem, w_in, w_in, w_in, gate_b, w_br_sgu, w_br_attn,
      w_br_mem, w_out)


def _ffn_kernel(x_ref, nffn_ref, wg_ref, wu_ref, wd_ref, o_ref, h_sc):
    j = pl.program_id(1)

    @pl.when(j == 0)
    def _():
        x = x_ref[...]
        h_sc[...] = _rms(x, nffn_ref[...]).astype(BF16)
        o_ref[...] = x

    h = h_sc[...]
    a = _dot(h, wg_ref[...])
    b = _dot(h, wu_ref[...])
    o_ref[...] += _dot((a * jax.nn.sigmoid(a) * b).astype(BF16), wd_ref[...])


def _ffn(x2d, norm_ffn, w_gate_up, w_down, tm, tf):
    t, d = x2d.shape
    d_ff = w_down.shape[0]
    nj = d_ff // tf
    row = lambda i, j: (i, 0)
    return pl.pallas_call(
        _ffn_kernel,
        grid=(t // tm, nj),
        in_specs=[
            pl.BlockSpec((tm, d), row),
            pl.BlockSpec((1, d), lambda i, j: (0, 0)),
            pl.BlockSpec((d, tf), lambda i, j: (0, j)),
            pl.BlockSpec((d, tf), lambda i, j: (0, nj + j)),
            pl.BlockSpec((tf, d), lambda i, j: (j, 0)),
        ],
        out_specs=pl.BlockSpec((tm, d), row),
        out_shape=jax.ShapeDtypeStruct((t, d), F32),
        scratch_shapes=[pltpu.VMEM((tm, d), BF16)],
        compiler_params=_params(("arbitrary", "arbitrary")),
        name="ffn",
    )(x2d, norm_ffn, w_gate_up, w_gate_up, w_down)


def _rope_tables(seq):
    quarter = HEAD_DIM // 4
    inv = ROPE_THETA ** (-jnp.arange(0, 2 * quarter, 2, dtype=F32) / (2 * quarter))
    pos = jnp.arange(seq, dtype=jnp.int32)
    ar = (pos // GRID_W).astype(F32)[:, None] * inv[None, :]
    ac = (pos % GRID_W).astype(F32)[:, None] * inv[None, :]
    cos_t = jnp.concatenate([jnp.cos(ar), jnp.cos(ar), jnp.cos(ac), jnp.cos(ac)], axis=-1)
    sin_t = jnp.concatenate([-jnp.sin(ar), jnp.sin(ar), -jnp.sin(ac), jnp.sin(ac)], axis=-1)
    return cos_t, sin_t


def _tile(n, want):
    if n <= want:
        return n
    for cand in range(want, 0, -128):
        if n % cand == 0:
            return cand
    return n


def kernel(x, mem, norm_mix, w_in, gate_b, sgu_norm, sgu_w, sgu_b, q_norm, k_norm, mem_norm,
           w_mem_kv, mq_norm, mk_norm, w_br_sgu, w_br_attn, w_br_mem, w_out, norm_ffn,
           w_gate_up, w_down):
    batch, seq, d = x.shape
    n_mem = mem.shape[1]
    depth = w_in.shape[0]
    assert seq % SGU_CHUNK == 0 and seq % GRID_W == 0

    tm = _tile(seq, 512)
    tq = _tile(seq, 512)
    tk = _tile(seq, 1024)
    tq_fast = _tile(seq, 2048)
    tk_fast = _tile(seq, 2048)
    tn = _tile(d, 512)
    tf = _tile(w_down.shape[1], 512)

    cos_t, sin_t = _rope_tables(seq)
    x2d = x.reshape(batch * seq, d)
    mem2d = mem.reshape(batch * n_mem, d)
    row = lambda a: a.reshape(1, -1)

    for l in range(depth):
        w_in_l = w_in[l].astype(BF16)
        sgu_b_full = jnp.repeat(sgu_b[l].T, HEAD_DIM, axis=1)
        mk, mv = _memkv(mem2d, row(mem_norm[l]), w_mem_kv[l].astype(BF16), row(mk_norm[l]), n_mem)
        y_sgu, q, k, v_ext, y_mem, kn2 = _inproj(
            x2d, row(norm_mix[l]), w_in_l, cos_t, sin_t, row(sgu_norm[l]), sgu_w[l].astype(BF16),
            sgu_b_full, row(q_norm[l]), row(k_norm[l]), row(mq_norm[l]), mk, mv, seq, n_mem, tm)
        y_attn = _attention(q, k, v_ext, kn2, batch, seq, tm, tq, tk, tq_fast, tk_fast)
        x2d = _merge(x2d, row(norm_mix[l]), y_sgu, y_attn, y_mem, w_in_l, gate_b[l],
                     w_br_sgu[l].astype(BF16), w_br_attn[l].astype(BF16), w_br_mem[l].astype(BF16),
                     w_out[l].astype(BF16), tm, tn)
        x2d = _ffn(x2d, row(norm_ffn[l]), w_gate_up[l].astype(BF16), w_down[l].astype(BF16), tm, tf)
    return x2d.reshape(batch, seq, d)
```

```python
import math

import jax
import jax.numpy as jnp
from jax import lax
from jax.experimental import pallas as pl
from jax.experimental.pallas import tpu as pltpu

HEAD_DIM = 128
N_Q_HEADS = 8
N_KV_HEADS = 2
GQA_GROUP = N_Q_HEADS // N_KV_HEADS
N_MEM_HEADS = 4
SGU_GROUPS = 4
SGU_CHUNK = 128
GRID_W = 64
ROPE_THETA = 10000.0
N_BRANCH = 3
EPS = 1e-6

SGU_WIDTH = SGU_GROUPS * HEAD_DIM
Q_WIDTH = N_Q_HEADS * HEAD_DIM
KV_WIDTH = N_KV_HEADS * HEAD_DIM
MEM_WIDTH = N_MEM_HEADS * HEAD_DIM
MIX_WIDTH = 2 * SGU_WIDTH + Q_WIDTH + 2 * KV_WIDTH + MEM_WIDTH
OFF_Q = 2 * SGU_WIDTH
OFF_K = OFF_Q + Q_WIDTH
OFF_V = OFF_K + KV_WIDTH
OFF_MQ = OFF_V + KV_WIDTH

LOG2E = math.log2(math.e)
ATTN_SCALE = HEAD_DIM ** -0.5

V7X_VMEM_LIMIT_BYTES = 56 * 1024 * 1024
SUBLANES = 8

STAB_SLACK = 60.0
ROW_SUM_FLOOR = 2.0 ** -80
KEY_NORM_MARGIN = 1.01

BF16 = jnp.bfloat16
F32 = jnp.float32


def _dot(a, b):
    return jnp.dot(a, b, preferred_element_type=F32)


def _dot_nt(a, b):
    return lax.dot_general(a, b, (((1,), (1,)), ((), ())), preferred_element_type=F32)


def _rms(z, gain):
    return z * lax.rsqrt(jnp.mean(z * z, axis=-1, keepdims=True) + EPS) * gain


def _gelu(z):
    return 0.5 * z * (1.0 + lax.erf(z * math.sqrt(0.5)))


def _rope(z, cos, sin_signed, first_of_pair):
    partner = jnp.where(first_of_pair, pltpu.roll(z, 96, 1), pltpu.roll(z, 32, 1))
    return z * cos + partner * sin_signed


def _params(sem):
    return pltpu.CompilerParams(dimension_semantics=sem, vmem_limit_bytes=V7X_VMEM_LIMIT_BYTES)


def _memkv_kernel(mem_ref, mnorm_ref, w_ref, mkn_ref, mk_ref, mv_ref):
    hm = _rms(mem_ref[...], mnorm_ref[...]).astype(BF16)
    kv = _dot(hm, w_ref[...])
    for hh in range(N_MEM_HEADS):
        sl = slice(hh * HEAD_DIM, (hh + 1) * HEAD_DIM)
        mk_ref[:, sl] = _rms(kv[:, sl], mkn_ref[...]).astype(BF16)
    mv_ref[...] = kv[:, MEM_WIDTH:].astype(BF16)


def _memkv(mem2d, mem_norm, w_mem_kv, mk_norm, n_mem, layer):
    rows, d = mem2d.shape
    return pl.pallas_call(
        _memkv_kernel,
        grid=(rows // n_mem,),
        in_specs=[
            pl.BlockSpec((n_mem, d), lambda i: (i, 0)),
            pl.BlockSpec((1, d), lambda i: (0, 0)),
            pl.BlockSpec((None, d, 2 * MEM_WIDTH), lambda i: (layer, 0, 0)),
            pl.BlockSpec((1, HEAD_DIM), lambda i: (0, 0)),
        ],
        out_specs=[
            pl.BlockSpec((n_mem, MEM_WIDTH), lambda i: (i, 0)),
            pl.BlockSpec((n_mem, MEM_WIDTH), lambda i: (i, 0)),
        ],
        out_shape=[jax.ShapeDtypeStruct((rows, MEM_WIDTH), BF16)] * 2,
        compiler_params=_params(("arbitrary",)),
        name="memkv",
    )(mem2d, mem_norm, w_mem_kv, mk_norm)


def _inproj_kernel(x_ref, nmix_ref, w_ref, cos_ref, sin_ref, sgun_ref, sguw_ref, sgub_ref,
                   qn_ref, kn_ref, mqn_ref, mk_ref, mv_ref,
                   ysgu_ref, q_ref, k_ref, v_ref, ymem_ref, kn2_ref):
    tm = x_ref.shape[0]
    h = _rms(x_ref[...], nmix_ref[...]).astype(BF16)
    z = _dot(h, w_ref[...])

    zs = _gelu(z[:, 0:OFF_Q])
    for g in range(SGU_GROUPS):
        sl = slice(g * HEAD_DIM, (g + 1) * HEAD_DIM)
        u_g = zs[:, sl]
        vs_g = _rms(zs[:, SGU_WIDTH + g * HEAD_DIM:SGU_WIDTH + (g + 1) * HEAD_DIM],
                    sgun_ref[...]).astype(BF16)
        w_g = sguw_ref[g]
        for c in range(tm // SGU_CHUNK):
            rows = slice(c * SGU_CHUNK, (c + 1) * SGU_CHUNK)
            sp = _dot(w_g, vs_g[rows, :]) + sgub_ref[:, sl]
            ysgu_ref[rows, sl] = (u_g[rows, :] * sp).astype(BF16)

    cos = cos_ref[...]
    sin_signed = sin_ref[...]
    lane = lax.broadcasted_iota(jnp.int32, (tm, HEAD_DIM), 1)
    first_of_pair = (lane & 32) == 0
    zq = z[:, OFF_Q:OFF_K]
    for hh in range(N_Q_HEADS):
        sl = slice(hh * HEAD_DIM, (hh + 1) * HEAD_DIM)
        qh = _rope(_rms(zq[:, sl], qn_ref[...]), cos, sin_signed, first_of_pair)
        q_ref[:, sl] = (qh * (ATTN_SCALE * LOG2E)).astype(BF16)
    zkv = z[:, OFF_K:OFF_MQ]
    for hh in range(N_KV_HEADS):
        sl = slice(hh * HEAD_DIM, (hh + 1) * HEAD_DIM)
        kh = _rope(_rms(zkv[:, sl], kn_ref[...]), cos, sin_signed, first_of_pair)
        k_ref[:, sl] = kh.astype(BF16)
        kn2 = jnp.max(jnp.sum(kh * kh, axis=-1, keepdims=True), axis=0, keepdims=True)
        kn2_ref[:, sl] = jnp.broadcast_to(kn2, (kn2_ref.shape[0], HEAD_DIM))
        v_ref[:, 2 * hh * HEAD_DIM:(2 * hh + 1) * HEAD_DIM] = zkv[:, KV_WIDTH + hh * HEAD_DIM:
                                                                   KV_WIDTH + (hh + 1) * HEAD_DIM].astype(BF16)
        v_ref[:, (2 * hh + 1) * HEAD_DIM:(2 * hh + 2) * HEAD_DIM] = jnp.ones((tm, HEAD_DIM), BF16)

    zmq = z[:, OFF_MQ:MIX_WIDTH]
    for hh in range(N_MEM_HEADS):
        sl = slice(hh * HEAD_DIM, (hh + 1) * HEAD_DIM)
        mq = (_rms(zmq[:, sl], mqn_ref[...]) * (ATTN_SCALE * LOG2E)).astype(BF16)
        s = _dot_nt(mq, mk_ref[:, sl])
        p = jnp.exp2(s - jnp.max(s, axis=-1, keepdims=True))
        o = _dot(p.astype(BF16), mv_ref[:, sl])
        ymem_ref[:, sl] = (o / jnp.sum(p, axis=-1, keepdims=True)).astype(BF16)


def _inproj(x2d, norm_mix, w_in, cos_t, sin_t, sgu_norm, sgu_w, sgu_b_full, q_norm, k_norm,
            mq_norm, mk, mv, seq, n_mem, tm, layer):
    t, d = x2d.shape
    tiles_per_seq = seq // tm
    const = lambda i: (0, 0)
    row = lambda i: (i, 0)
    return pl.pallas_call(
        _inproj_kernel,
        grid=(t // tm,),
        in_specs=[
            pl.BlockSpec((tm, d), row),
            pl.BlockSpec((1, d), const),
            pl.BlockSpec((None, d, MIX_WIDTH), lambda i: (layer, 0, 0)),
            pl.BlockSpec((tm, HEAD_DIM), lambda i: (i % tiles_per_seq, 0)),
            pl.BlockSpec((tm, HEAD_DIM), lambda i: (i % tiles_per_seq, 0)),
            pl.BlockSpec((1, HEAD_DIM), const),
            pl.BlockSpec((None, SGU_GROUPS, SGU_CHUNK, SGU_CHUNK), lambda i: (layer, 0, 0, 0)),
            pl.BlockSpec((SGU_CHUNK, SGU_WIDTH), const),
            pl.BlockSpec((1, HEAD_DIM), const),
            pl.BlockSpec((1, HEAD_DIM), const),
            pl.BlockSpec((1, HEAD_DIM), const),
            pl.BlockSpec((n_mem, MEM_WIDTH), lambda i: (i // tiles_per_seq, 0)),
            pl.BlockSpec((n_mem, MEM_WIDTH), lambda i: (i // tiles_per_seq, 0)),
        ],
        out_specs=[
            pl.BlockSpec((tm, SGU_WIDTH), row),
            pl.BlockSpec((tm, Q_WIDTH), row),
            pl.BlockSpec((tm, KV_WIDTH), row),
            pl.BlockSpec((tm, 2 * KV_WIDTH), row),
            pl.BlockSpec((tm, MEM_WIDTH), row),
            pl.BlockSpec((SUBLANES, KV_WIDTH), row),
        ],
        out_shape=[
            jax.ShapeDtypeStruct((t, SGU_WIDTH), BF16),
            jax.ShapeDtypeStruct((t, Q_WIDTH), BF16),
            jax.ShapeDtypeStruct((t, KV_WIDTH), BF16),
            jax.ShapeDtypeStruct((t, 2 * KV_WIDTH), BF16),
            jax.ShapeDtypeStruct((t, MEM_WIDTH), BF16),
            jax.ShapeDtypeStruct((t // tm * SUBLANES, KV_WIDTH), F32),
        ],
        compiler_params=_params(("arbitrary",)),
        name="inproj",
    )(x2d, norm_mix, w_in, cos_t, sin_t, sgu_norm, sgu_w, sgu_b_full, q_norm, k_norm, mq_norm, mk, mv)


def _flash_kernel(q_ref, k_ref, v_ref, o_ref, m_sc, l_sc, acc_sc):
    ki = pl.program_id(3)

    @pl.when(ki == 0)
    def _():
        m_sc[...] = jnp.full(m_sc.shape, -jnp.inf, F32)
        l_sc[...] = jnp.zeros(l_sc.shape, F32)
        acc_sc[...] = jnp.zeros(acc_sc.shape, F32)

    k = k_ref[...]
    v = v_ref[...]
    for hh in range(GQA_GROUP):
        sl = slice(hh * HEAD_DIM, (hh + 1) * HEAD_DIM)
        s = _dot_nt(q_ref[:, sl], k)
        m_prev = m_sc[hh]
        m_new = jnp.maximum(m_prev, jnp.max(s, axis=-1, keepdims=True))
        alpha = jnp.exp2(m_prev - m_new)
        p = jnp.exp2(s - m_new[:, 0:1])
        l_sc[hh] = alpha * l_sc[hh] + jnp.sum(p, axis=-1, keepdims=True)
        acc_sc[hh] = alpha * acc_sc[hh] + _dot(p.astype(BF16), v)
        m_sc[hh] = m_new

    @pl.when(ki == pl.num_programs(3) - 1)
    def _():
        for hh in range(GQA_GROUP):
            sl = slice(hh * HEAD_DIM, (hh + 1) * HEAD_DIM)
            o_ref[:, sl] = (acc_sc[hh] / l_sc[hh]).astype(BF16)


def _flash_fast_kernel(kmax_ref, q_ref, k_ref, v_ref, o_ref, lmin_ref, c_sc, acc_sc):
    ki = pl.program_id(3)
    tq = q_ref.shape[0]
    tk = k_ref.shape[0]

    @pl.when(ki == 0)
    def _():
        kmax = kmax_ref[pl.program_id(0), pl.program_id(1)]
        for hh in range(GQA_GROUP):
            qf = q_ref[:, hh * HEAD_DIM:(hh + 1) * HEAD_DIM].astype(F32)
            qn = jnp.sqrt(jnp.sum(qf * qf, axis=-1, keepdims=True))
            c_sc[hh] = jnp.broadcast_to(qn * kmax - STAB_SLACK, (tq, HEAD_DIM))
        acc_sc[...] = jnp.zeros(acc_sc.shape, F32)

    k = k_ref[...]
    v = v_ref[...]
    for hh in range(GQA_GROUP):
        s = _dot_nt(q_ref[:, hh * HEAD_DIM:(hh + 1) * HEAD_DIM], k)
        p = jnp.exp2(s - jnp.tile(c_sc[hh], (1, tk // HEAD_DIM)))
        acc_sc[hh] += _dot(p.astype(BF16), v)

    @pl.when(ki == pl.num_programs(3) - 1)
    def _():
        lmin = None
        for hh in range(GQA_GROUP):
            acc = acc_sc[hh]
            l = acc[:, HEAD_DIM:]
            o_ref[:, hh * HEAD_DIM:(hh + 1) * HEAD_DIM] = (acc[:, :HEAD_DIM] / l).astype(BF16)
            lm = jnp.min(l, axis=0, keepdims=True)
            lmin = lm if lmin is None else jnp.minimum(lmin, lm)
        lmin_ref[...] = jnp.broadcast_to(lmin, lmin_ref.shape)


def _flash_fast(kmax, q, k, v_ext, batch, seq, tq, tk):
    t = q.shape[0]
    nq = seq // tq
    nk = seq // tk
    group_w = GQA_GROUP * HEAD_DIM
    return pl.pallas_call(
        _flash_fast_kernel,
        grid=(batch, N_KV_HEADS, nq, nk),
        in_specs=[
            pl.BlockSpec(memory_space=pltpu.SMEM),
            pl.BlockSpec((tq, group_w), lambda b, g, qi, ki: (b * nq + qi, g)),
            pl.BlockSpec((tk, HEAD_DIM), lambda b, g, qi, ki: (b * nk + ki, g)),
            pl.BlockSpec((tk, 2 * HEAD_DIM), lambda b, g, qi, ki: (b * nk + ki, g)),
        ],
        out_specs=[
            pl.BlockSpec((tq, group_w), lambda b, g, qi, ki: (b * nq + qi, g)),
            pl.BlockSpec((SUBLANES, HEAD_DIM), lambda b, g, qi, ki: (b * nq + qi, g)),
        ],
        out_shape=[
            jax.ShapeDtypeStruct((t, Q_WIDTH), BF16),
            jax.ShapeDtypeStruct((batch * nq * SUBLANES, N_KV_HEADS * HEAD_DIM), F32),
        ],
        scratch_shapes=[
            pltpu.VMEM((GQA_GROUP, tq, HEAD_DIM), F32),
            pltpu.VMEM((GQA_GROUP, tq, 2 * HEAD_DIM), F32),
        ],
        compiler_params=_params(("arbitrary", "arbitrary", "arbitrary", "arbitrary")),
        name="flash_fast",
    )(kmax, q, k, v_ext)


def _flash(q, k, v_ext, batch, seq, tq, tk):
    t = q.shape[0]
    nq = seq // tq
    nk = seq // tk
    group_w = GQA_GROUP * HEAD_DIM
    return pl.pallas_call(
        _flash_kernel,
        grid=(batch, N_KV_HEADS, nq, nk),
        in_specs=[
            pl.BlockSpec((tq, group_w), lambda b, g, qi, ki: (b * nq + qi, g)),
            pl.BlockSpec((tk, HEAD_DIM), lambda b, g, qi, ki: (b * nk + ki, g)),
            pl.BlockSpec((tk, HEAD_DIM), lambda b, g, qi, ki: (b * nk + ki, 2 * g)),
        ],
        out_specs=pl.BlockSpec((tq, group_w), lambda b, g, qi, ki: (b * nq + qi, g)),
        out_shape=jax.ShapeDtypeStruct((t, Q_WIDTH), BF16),
        scratch_shapes=[
            pltpu.VMEM((GQA_GROUP, tq, HEAD_DIM), F32),
            pltpu.VMEM((GQA_GROUP, tq, HEAD_DIM), F32),
            pltpu.VMEM((GQA_GROUP, tq, HEAD_DIM), F32),
        ],
        compiler_params=_params(("arbitrary", "arbitrary", "arbitrary", "arbitrary")),
        name="flash",
    )(q, k, v_ext)


def _attention(q, k, v_ext, kn2, batch, seq, tm, tq, tk, tq_fast, tk_fast):
    kmax = jnp.sqrt(jnp.max(
        kn2.reshape(batch, seq // tm, SUBLANES, N_KV_HEADS, HEAD_DIM)[:, :, 0, :, 0], axis=1))
    y_fast, lmin = _flash_fast(kmax * KEY_NORM_MARGIN, q, k, v_ext, batch, seq, tq_fast, tk_fast)
    trusted = jnp.all(lmin >= ROW_SUM_FLOOR)
    return lax.cond(trusted, lambda: y_fast, lambda: _flash(q, k, v_ext, batch, seq, tq, tk))


def _merge_kernel(x_ref, nmix_ref, ysgu_ref, yattn_ref, ymem_ref, wg0_ref, wg1_ref, wg2_ref,
                  gb_ref, wbs_ref, wba_ref, wbm_ref, wout_ref, o_ref, h_sc):
    j = pl.program_id(1)

    @pl.when(j == 0)
    def _():
        x = x_ref[...]
        h_sc[...] = _rms(x, nmix_ref[...]).astype(BF16)
        o_ref[...] = x

    h = h_sc[...]
    y = jax.nn.sigmoid(_dot(h, wg0_ref[...]) + gb_ref[0:1, :]) * _dot(ysgu_ref[...], wbs_ref[...])
    y += jax.nn.sigmoid(_dot(h, wg1_ref[...]) + gb_ref[1:2, :]) * _dot(yattn_ref[...], wba_ref[...])
    y += jax.nn.sigmoid(_dot(h, wg2_ref[...]) + gb_ref[2:3, :]) * _dot(ymem_ref[...], wbm_ref[...])
    o_ref[...] += _dot(y.astype(BF16), wout_ref[...])


def _merge(x2d, norm_mix, y_sgu, y_attn, y_mem, w_in, gate_b, w_br_sgu, w_br_attn, w_br_mem,
           w_out, tm, tn, layer):
    t, d = x2d.shape
    nj = d // tn
    gate0 = MIX_WIDTH // tn
    row = lambda i, j: (i, 0)
    col = lambda i, j: (0, j)
    lcol = lambda i, j: (layer, 0, j)
    return pl.pallas_call(
        _merge_kernel,
        grid=(t // tm, nj),
        in_specs=[
            pl.BlockSpec((tm, d), row),
            pl.BlockSpec((1, d), lambda i, j: (0, 0)),
            pl.BlockSpec((tm, SGU_WIDTH), row),
            pl.BlockSpec((tm, Q_WIDTH), row),
            pl.BlockSpec((tm, MEM_WIDTH), row),
            pl.BlockSpec((None, d, tn), lambda i, j: (layer, 0, gate0 + j)),
            pl.BlockSpec((None, d, tn), lambda i, j: (layer, 0, gate0 + nj + j)),
            pl.BlockSpec((None, d, tn), lambda i, j: (layer, 0, gate0 + 2 * nj + j)),
            pl.BlockSpec((N_BRANCH, tn), col),
            pl.BlockSpec((None, SGU_WIDTH, tn), lcol),
            pl.BlockSpec((None, Q_WIDTH, tn), lcol),
            pl.BlockSpec((None, MEM_WIDTH, tn), lcol),
            pl.BlockSpec((None, tn, d), lambda i, j: (layer, j, 0)),
        ],
        out_specs=pl.BlockSpec((tm, d), row),
        out_shape=jax.ShapeDtypeStruct((t, d), F32),
        scratch_shapes=[pltpu.VMEM((tm, d), BF16)],
        compiler_params=_params(("arbitrary", "arbitrary")),
        name="merge",
    )(x2d, norm_mix, y_sgu, y_attn, y_mem, w_in, w_in, w_in, gate_b, w_br_sgu, w_br_attn,
      w_br_mem, w_out)


def _ffn_kernel(x_ref, nffn_ref, wg_ref, wu_ref, wd_ref, o_ref, h_sc):
    j = pl.program_id(1)

    @pl.when(j == 0)
    def _():
        x = x_ref[...]
        h_sc[...] = _rms(x, nffn_ref[...]).astype(BF16)
        o_ref[...] = x

    h = h_sc[...]
    a = _dot(h, wg_ref[...])
    b = _dot(h, wu_ref[...])
    o_ref[...] += _dot((a * jax.nn.sigmoid(a) * b).astype(BF16), wd_ref[...])


def _ffn(x2d, norm_ffn, w_gate_up, w_down, tm, tf, layer):
    t, d = x2d.shape
    d_ff = w_down.shape[1]
    nj = d_ff // tf
    row = lambda i, j: (i, 0)
    return pl.pallas_call(
        _ffn_kernel,
        grid=(t // tm, nj),
        in_specs=[
            pl.BlockSpec((tm, d), row),
            pl.BlockSpec((1, d), lambda i, j: (0, 0)),
            pl.BlockSpec((None, d, tf), lambda i, j: (layer, 0, j)),
            pl.BlockSpec((None, d, tf), lambda i, j: (layer, 0, nj + j)),
            pl.BlockSpec((None, tf, d), lambda i, j: (layer, j, 0)),
        ],
        out_specs=pl.BlockSpec((tm, d), row),
        out_shape=jax.ShapeDtypeStruct((t, d), F32),
        scratch_shapes=[pltpu.VMEM((tm, d), BF16)],
        compiler_params=_params(("arbitrary", "arbitrary")),
        name="ffn",
    )(x2d, norm_ffn, w_gate_up, w_gate_up, w_down)


def _rope_tables(seq):
    quarter = HEAD_DIM // 4
    inv = ROPE_THETA ** (-jnp.arange(0, 2 * quarter, 2, dtype=F32) / (2 * quarter))
    pos = jnp.arange(seq, dtype=jnp.int32)
    ar = (pos // GRID_W).astype(F32)[:, None] * inv[None, :]
    ac = (pos % GRID_W).astype(F32)[:, None] * inv[None, :]
    cos_t = jnp.concatenate([jnp.cos(ar), jnp.cos(ar), jnp.cos(ac), jnp.cos(ac)], axis=-1)
    sin_t = jnp.concatenate([-jnp.sin(ar), jnp.sin(ar), -jnp.sin(ac), jnp.sin(ac)], axis=-1)
    return cos_t, sin_t


def _tile(n, want):
    if n <= want:
        return n
    for cand in range(want, 0, -128):
        if n % cand == 0:
            return cand
    return n


def kernel(x, mem, norm_mix, w_in, gate_b, sgu_norm, sgu_w, sgu_b, q_norm, k_norm, mem_norm,
           w_mem_kv, mq_norm, mk_norm, w_br_sgu, w_br_attn, w_br_mem, w_out, norm_ffn,
           w_gate_up, w_down):
    batch, seq, d = x.shape
    n_mem = mem.shape[1]
    depth = w_in.shape[0]
    assert seq % SGU_CHUNK == 0 and seq % GRID_W == 0

    tm = _tile(seq, 512)
    tq = _tile(seq, 512)
    tk = _tile(seq, 1024)
    tq_fast = _tile(seq, 2048)
    tk_fast = _tile(seq, 2048)
    tn = _tile(d, 512)
    tm_ffn = _tile(seq, 1024)
    tf = _tile(w_down.shape[1], 512)

    cos_t, sin_t = _rope_tables(seq)
    x2d = x.reshape(batch * seq, d)
    mem2d = mem.reshape(batch * n_mem, d)
    row = lambda a: a.reshape(1, -1)

    w_in_b, w_mem_kv_b, sgu_w_b = w_in.astype(BF16), w_mem_kv.astype(BF16), sgu_w.astype(BF16)
    w_br_sgu_b, w_br_attn_b, w_br_mem_b = (w.astype(BF16) for w in (w_br_sgu, w_br_attn, w_br_mem))
    w_out_b, w_gate_up_b, w_down_b = w_out.astype(BF16), w_gate_up.astype(BF16), w_down.astype(BF16)

    for l in range(depth):
        sgu_b_full = jnp.repeat(sgu_b[l].T, HEAD_DIM, axis=1)
        mk, mv = _memkv(mem2d, row(mem_norm[l]), w_mem_kv_b, row(mk_norm[l]), n_mem, l)
        y_sgu, q, k, v_ext, y_mem, kn2 = _inproj(
            x2d, row(norm_mix[l]), w_in_b, cos_t, sin_t, row(sgu_norm[l]), sgu_w_b, sgu_b_full,
            row(q_norm[l]), row(k_norm[l]), row(mq_norm[l]), mk, mv, seq, n_mem, tm, l)
        y_attn = _attention(q, k, v_ext, kn2, batch, seq, tm, tq, tk, tq_fast, tk_fast)
        x2d = _merge(x2d, row(norm_mix[l]), y_sgu, y_attn, y_mem, w_in_b, gate_b[l], w_br_sgu_b,
                     w_br_attn_b, w_br_mem_b, w_out_b, tm, tn, l)
        x2d = _ffn(x2d, row(norm_ffn[l]), w_gate_up_b, w_down_b, tm_ffn, tf, l)
    return x2d.reshape(batch, seq, d)
```

```python
import math

import jax
import jax.numpy as jnp
from jax import lax
from jax.experimental import pallas as pl
from jax.experimental.pallas import tpu as pltpu

HEAD_DIM = 128
N_Q_HEADS = 8
N_KV_HEADS = 2
GQA_GROUP = N_Q_HEADS // N_KV_HEADS
N_MEM_HEADS = 4
SGU_GROUPS = 4
SGU_CHUNK = 128
GRID_W = 64
ROPE_THETA = 10000.0
N_BRANCH = 3
EPS = 1e-6

SGU_WIDTH = SGU_GROUPS * HEAD_DIM
Q_WIDTH = N_Q_HEADS * HEAD_DIM
KV_WIDTH = N_KV_HEADS * HEAD_DIM
MEM_WIDTH = N_MEM_HEADS * HEAD_DIM
MIX_WIDTH = 2 * SGU_WIDTH + Q_WIDTH + 2 * KV_WIDTH + MEM_WIDTH
OFF_Q = 2 * SGU_WIDTH
OFF_K = OFF_Q + Q_WIDTH
OFF_V = OFF_K + KV_WIDTH
OFF_MQ = OFF_V + KV_WIDTH

LOG2E = math.log2(math.e)
ATTN_SCALE = HEAD_DIM ** -0.5

V7X_VMEM_LIMIT_BYTES = 56 * 1024 * 1024
SUBLANES = 8

STAB_SLACK = 60.0
ROW_SUM_FLOOR = 2.0 ** -80
KEY_NORM_MARGIN = 1.01

BF16 = jnp.bfloat16
F32 = jnp.float32


def _dot(a, b):
    return jnp.dot(a, b, preferred_element_type=F32)


def _dot_nt(a, b):
    return lax.dot_general(a, b, (((1,), (1,)), ((), ())), preferred_element_type=F32)


def _rms(z, gain):
    return z * lax.rsqrt(jnp.mean(z * z, axis=-1, keepdims=True) + EPS) * gain


def _gelu(z):
    return 0.5 * z * (1.0 + lax.erf(z * math.sqrt(0.5)))


def _rope(z, cos, sin_signed, first_of_pair):
    partner = jnp.where(first_of_pair, pltpu.roll(z, 96, 1), pltpu.roll(z, 32, 1))
    return z * cos + partner * sin_signed


def _params(sem):
    return pltpu.CompilerParams(dimension_semantics=sem, vmem_limit_bytes=V7X_VMEM_LIMIT_BYTES)


def _memkv_kernel(mem_ref, mnorm_ref, w_ref, mkn_ref, mk_ref, mv_ref):
    hm = _rms(mem_ref[...], mnorm_ref[...]).astype(BF16)
    kv = _dot(hm, w_ref[...])
    for hh in range(N_MEM_HEADS):
        sl = slice(hh * HEAD_DIM, (hh + 1) * HEAD_DIM)
        mk_ref[:, sl] = _rms(kv[:, sl], mkn_ref[...]).astype(BF16)
    mv_ref[...] = kv[:, MEM_WIDTH:].astype(BF16)


def _memkv(mem2d, mem_norm, w_mem_kv, mk_norm, n_mem, layer):
    rows, d = mem2d.shape
    return pl.pallas_call(
        _memkv_kernel,
        grid=(rows // n_mem,),
        in_specs=[
            pl.BlockSpec((n_mem, d), lambda i: (i, 0)),
            pl.BlockSpec((1, d), lambda i: (0, 0)),
            pl.BlockSpec((None, d, 2 * MEM_WIDTH), lambda i: (layer, 0, 0)),
            pl.BlockSpec((1, HEAD_DIM), lambda i: (0, 0)),
        ],
        out_specs=[
            pl.BlockSpec((n_mem, MEM_WIDTH), lambda i: (i, 0)),
            pl.BlockSpec((n_mem, MEM_WIDTH), lambda i: (i, 0)),
        ],
        out_shape=[jax.ShapeDtypeStruct((rows, MEM_WIDTH), BF16)] * 2,
        compiler_params=_params(("arbitrary",)),
        name="memkv",
    )(mem2d, mem_norm, w_mem_kv, mk_norm)


def _inproj_kernel(x_ref, nmix_ref, w_ref, cos_ref, sin_ref, sgun_ref, sguw_ref, sgub_ref,
                   qn_ref, kn_ref, mqn_ref, mk_ref, mv_ref,
                   ysgu_ref, q_ref, k_ref, v_ref, ymem_ref, kn2_ref, za_sc, zb_sc):
    i = pl.program_id(0)
    rest = (cos_ref, sin_ref, sgun_ref, sguw_ref, sgub_ref, qn_ref, kn_ref, mqn_ref, mk_ref, mv_ref,
            ysgu_ref, q_ref, k_ref, v_ref, ymem_ref, kn2_ref)

    @pl.when(i == 0)
    def _():
        zb_sc[...] = jnp.zeros(zb_sc.shape, F32)

    def step(z_write, z_read):
        h = _rms(x_ref[...], nmix_ref[...]).astype(BF16)
        z_write[...] = _dot(h, w_ref[...])
        _inproj_finish(z_read, *rest)

    @pl.when(i % 2 == 0)
    def _():
        step(za_sc, zb_sc)

    @pl.when(i % 2 == 1)
    def _():
        step(zb_sc, za_sc)


def _inproj_finish(z, cos_ref, sin_ref, sgun_ref, sguw_ref, sgub_ref, qn_ref, kn_ref, mqn_ref,
                   mk_ref, mv_ref, ysgu_ref, q_ref, k_ref, v_ref, ymem_ref, kn2_ref):
    tm = z.shape[0]

    zs = _gelu(z[:, 0:OFF_Q])
    for g in range(SGU_GROUPS):
        sl = slice(g * HEAD_DIM, (g + 1) * HEAD_DIM)
        u_g = zs[:, sl]
        vs_g = _rms(zs[:, SGU_WIDTH + g * HEAD_DIM:SGU_WIDTH + (g + 1) * HEAD_DIM],
                    sgun_ref[...]).astype(BF16)
        w_g = sguw_ref[g]
        for c in range(tm // SGU_CHUNK):
            rows = slice(c * SGU_CHUNK, (c + 1) * SGU_CHUNK)
            sp = _dot(w_g, vs_g[rows, :]) + sgub_ref[:, sl]
            ysgu_ref[rows, sl] = (u_g[rows, :] * sp).astype(BF16)

    cos = cos_ref[...]
    sin_signed = sin_ref[...]
    lane = lax.broadcasted_iota(jnp.int32, (tm, HEAD_DIM), 1)
    first_of_pair = (lane & 32) == 0
    zq = z[:, OFF_Q:OFF_K]
    for hh in range(N_Q_HEADS):
        sl = slice(hh * HEAD_DIM, (hh + 1) * HEAD_DIM)
        qh = _rope(_rms(zq[:, sl], qn_ref[...]), cos, sin_signed, first_of_pair)
        q_ref[:, sl] = (qh * (ATTN_SCALE * LOG2E)).astype(BF16)
    zkv = z[:, OFF_K:OFF_MQ]
    for hh in range(N_KV_HEADS):
        sl = slice(hh * HEAD_DIM, (hh + 1) * HEAD_DIM)
        kh = _rope(_rms(zkv[:, sl], kn_ref[...]), cos, sin_signed, first_of_pair)
        k_ref[:, sl] = kh.astype(BF16)
        kn2 = jnp.max(jnp.sum(kh * kh, axis=-1, keepdims=True), axis=0, keepdims=True)
        kn2_ref[:, sl] = jnp.broadcast_to(kn2, (kn2_ref.shape[0], HEAD_DIM))
        v_ref[:, 2 * hh * HEAD_DIM:(2 * hh + 1) * HEAD_DIM] = zkv[:, KV_WIDTH + hh * HEAD_DIM:
                                                                   KV_WIDTH + (hh + 1) * HEAD_DIM].astype(BF16)
        v_ref[:, (2 * hh + 1) * HEAD_DIM:(2 * hh + 2) * HEAD_DIM] = jnp.ones((tm, HEAD_DIM), BF16)

    zmq = z[:, OFF_MQ:MIX_WIDTH]
    for hh in range(N_MEM_HEADS):
        sl = slice(hh * HEAD_DIM, (hh + 1) * HEAD_DIM)
        mq = (_rms(zmq[:, sl], mqn_ref[...]) * (ATTN_SCALE * LOG2E)).astype(BF16)
        s = _dot_nt(mq, mk_ref[:, sl])
        p = jnp.exp2(s - jnp.max(s, axis=-1, keepdims=True))
        o = _dot(p.astype(BF16), mv_ref[:, sl])
        ymem_ref[:, sl] = (o / jnp.sum(p, axis=-1, keepdims=True)).astype(BF16)


def _inproj(x2d, norm_mix, w_in, cos_t, sin_t, sgu_norm, sgu_w, sgu_b_full, q_norm, k_norm,
            mq_norm, mk, mv, seq, n_mem, tm, layer):
    t, d = x2d.shape
    tiles_per_seq = seq // tm
    n_tiles = t // tm
    const = lambda i: (0, 0)
    prev = lambda i: jnp.maximum(i - 1, 0)
    row = lambda i: (prev(i), 0)
    return pl.pallas_call(
        _inproj_kernel,
        grid=(n_tiles + 1,),
        in_specs=[
            pl.BlockSpec((tm, d), lambda i: (jnp.minimum(i, n_tiles - 1), 0)),
            pl.BlockSpec((1, d), const),
            pl.BlockSpec((None, d, MIX_WIDTH), lambda i: (layer, 0, 0), pipeline_mode=pl.Buffered(1)),
            pl.BlockSpec((tm, HEAD_DIM), lambda i: (prev(i) % tiles_per_seq, 0)),
            pl.BlockSpec((tm, HEAD_DIM), lambda i: (prev(i) % tiles_per_seq, 0)),
            pl.BlockSpec((1, HEAD_DIM), const),
            pl.BlockSpec((None, SGU_GROUPS, SGU_CHUNK, SGU_CHUNK), lambda i: (layer, 0, 0, 0)),
            pl.BlockSpec((SGU_CHUNK, SGU_WIDTH), const),
            pl.BlockSpec((1, HEAD_DIM), const),
            pl.BlockSpec((1, HEAD_DIM), const),
            pl.BlockSpec((1, HEAD_DIM), const),
            pl.BlockSpec((n_mem, MEM_WIDTH), lambda i: (prev(i) // tiles_per_seq, 0)),
            pl.BlockSpec((n_mem, MEM_WIDTH), lambda i: (prev(i) // tiles_per_seq, 0)),
        ],
        out_specs=[
            pl.BlockSpec((tm, SGU_WIDTH), row),
            pl.BlockSpec((tm, Q_WIDTH), row),
            pl.BlockSpec((tm, KV_WIDTH), row),
            pl.BlockSpec((tm, 2 * KV_WIDTH), row),
            pl.BlockSpec((tm, MEM_WIDTH), row),
            pl.BlockSpec((SUBLANES, KV_WIDTH), row),
        ],
        out_shape=[
            jax.ShapeDtypeStruct((t, SGU_WIDTH), BF16),
            jax.ShapeDtypeStruct((t, Q_WIDTH), BF16),
            jax.ShapeDtypeStruct((t, KV_WIDTH), BF16),
            jax.ShapeDtypeStruct((t, 2 * KV_WIDTH), BF16),
            jax.ShapeDtypeStruct((t, MEM_WIDTH), BF16),
            jax.ShapeDtypeStruct((t // tm * SUBLANES, KV_WIDTH), F32),
        ],
        scratch_shapes=[pltpu.VMEM((tm, MIX_WIDTH), F32), pltpu.VMEM((tm, MIX_WIDTH), F32)],
        compiler_params=_params(("arbitrary",)),
        name="inproj",
    )(x2d, norm_mix, w_in, cos_t, sin_t, sgu_norm, sgu_w, sgu_b_full, q_norm, k_norm, mq_norm, mk, mv)


def _flash_kernel(q_ref, k_ref, v_ref, o_ref, m_sc, l_sc, acc_sc):
    ki = pl.program_id(3)

    @pl.when(ki == 0)
    def _():
        m_sc[...] = jnp.full(m_sc.shape, -jnp.inf, F32)
        l_sc[...] = jnp.zeros(l_sc.shape, F32)
        acc_sc[...] = jnp.zeros(acc_sc.shape, F32)

    k = k_ref[...]
    v = v_ref[...]
    for hh in range(GQA_GROUP):
        sl = slice(hh * HEAD_DIM, (hh + 1) * HEAD_DIM)
        s = _dot_nt(q_ref[:, sl], k)
        m_prev = m_sc[hh]
        m_new = jnp.maximum(m_prev, jnp.max(s, axis=-1, keepdims=True))
        alpha = jnp.exp2(m_prev - m_new)
        p = jnp.exp2(s - m_new[:, 0:1])
        l_sc[hh] = alpha * l_sc[hh] + jnp.sum(p, axis=-1, keepdims=True)
        acc_sc[hh] = alpha * acc_sc[hh] + _dot(p.astype(BF16), v)
        m_sc[hh] = m_new

    @pl.when(ki == pl.num_programs(3) - 1)
    def _():
        for hh in range(GQA_GROUP):
            sl = slice(hh * HEAD_DIM, (hh + 1) * HEAD_DIM)
            o_ref[:, sl] = (acc_sc[hh] / l_sc[hh]).astype(BF16)


def _flash_fast_kernel(kmax_ref, q_ref, k_ref, v_ref, o_ref, lmin_ref, c_sc, acc_sc):
    ki = pl.program_id(3)
    tq = q_ref.shape[0]
    tk = k_ref.shape[0]

    @pl.when(ki == 0)
    def _():
        kmax = kmax_ref[pl.program_id(0), pl.program_id(1)]
        for hh in range(GQA_GROUP):
            qf = q_ref[:, hh * HEAD_DIM:(hh + 1) * HEAD_DIM].astype(F32)
            qn = jnp.sqrt(jnp.sum(qf * qf, axis=-1, keepdims=True))
            c_sc[hh] = jnp.broadcast_to(qn * kmax - STAB_SLACK, (tq, HEAD_DIM))
        acc_sc[...] = jnp.zeros(acc_sc.shape, F32)

    k = k_ref[...]
    v = v_ref[...]
    for hh in range(GQA_GROUP):
        s = _dot_nt(q_ref[:, hh * HEAD_DIM:(hh + 1) * HEAD_DIM], k)
        p = jnp.exp2(s - jnp.tile(c_sc[hh], (1, tk // HEAD_DIM)))
        acc_sc[hh] += _dot(p.astype(BF16), v)

    @pl.when(ki == pl.num_programs(3) - 1)
    def _():
        lmin = None
        for hh in range(GQA_GROUP):
            acc = acc_sc[hh]
            l = acc[:, HEAD_DIM:]
            o_ref[:, hh * HEAD_DIM:(hh + 1) * HEAD_DIM] = (acc[:, :HEAD_DIM] / l).astype(BF16)
            lm = jnp.min(l, axis=0, keepdims=True)
            lmin = lm if lmin is None else jnp.minimum(lmin, lm)
        lmin_ref[...] = jnp.broadcast_to(lmin, lmin_ref.shape)


def _flash_fast(kmax, q, k, v_ext, batch, seq, tq, tk):
    t = q.shape[0]
    nq = seq // tq
    nk = seq // tk
    group_w = GQA_GROUP * HEAD_DIM
    return pl.pallas_call(
        _flash_fast_kernel,
        grid=(batch, N_KV_HEADS, nq, nk),
        in_specs=[
            pl.BlockSpec(memory_space=pltpu.SMEM),
            pl.BlockSpec((tq, group_w), lambda b, g, qi, ki: (b * nq + qi, g)),
            pl.BlockSpec((tk, HEAD_DIM), lambda b, g, qi, ki: (b * nk + ki, g)),
            pl.BlockSpec((tk, 2 * HEAD_DIM), lambda b, g, qi, ki: (b * nk + ki, g)),
        ],
        out_specs=[
            pl.BlockSpec((tq, group_w), lambda b, g, qi, ki: (b * nq + qi, g)),
            pl.BlockSpec((SUBLANES, HEAD_DIM), lambda b, g, qi, ki: (b * nq + qi, g)),
        ],
        out_shape=[
            jax.ShapeDtypeStruct((t, Q_WIDTH), BF16),
            jax.ShapeDtypeStruct((batch * nq * SUBLANES, N_KV_HEADS * HEAD_DIM), F32),
        ],
        scratch_shapes=[
            pltpu.VMEM((GQA_GROUP, tq, HEAD_DIM), F32),
            pltpu.VMEM((GQA_GROUP, tq, 2 * HEAD_DIM), F32),
        ],
        compiler_params=_params(("arbitrary", "arbitrary", "arbitrary", "arbitrary")),
        name="flash_fast",
    )(kmax, q, k, v_ext)


def _flash(q, k, v_ext, batch, seq, tq, tk):
    t = q.shape[0]
    nq = seq // tq
    nk = seq // tk
    group_w = GQA_GROUP * HEAD_DIM
    return pl.pallas_call(
        _flash_kernel,
        grid=(batch, N_KV_HEADS, nq, nk),
        in_specs=[
            pl.BlockSpec((tq, group_w), lambda b, g, qi, ki: (b * nq + qi, g)),
            pl.BlockSpec((tk, HEAD_DIM), lambda b, g, qi, ki: (b * nk + ki, g)),
            pl.BlockSpec((tk, HEAD_DIM), lambda b, g, qi, ki: (b * nk + ki, 2 * g)),
        ],
        out_specs=pl.BlockSpec((tq, group_w), lambda b, g, qi, ki: (b * nq + qi, g)),
        out_shape=jax.ShapeDtypeStruct((t, Q_WIDTH), BF16),
        scratch_shapes=[
            pltpu.VMEM((GQA_GROUP, tq, HEAD_DIM), F32),
            pltpu.VMEM((GQA_GROUP, tq, HEAD_DIM), F32),
            pltpu.VMEM((GQA_GROUP, tq, HEAD_DIM), F32),
        ],
        compiler_params=_params(("arbitrary", "arbitrary", "arbitrary", "arbitrary")),
        name="flash",
    )(q, k, v_ext)


def _attention(q, k, v_ext, kn2, batch, seq, tm, tq, tk, tq_fast, tk_fast):
    kmax = jnp.sqrt(jnp.max(
        kn2.reshape(batch, seq // tm, SUBLANES, N_KV_HEADS, HEAD_DIM)[:, :, 0, :, 0], axis=1))
    y_fast, lmin = _flash_fast(kmax * KEY_NORM_MARGIN, q, k, v_ext, batch, seq, tq_fast, tk_fast)
    trusted = jnp.all(lmin >= ROW_SUM_FLOOR)
    return lax.cond(trusted, lambda: y_fast, lambda: _flash(q, k, v_ext, batch, seq, tq, tk))


def _merge_kernel(x_ref, nmix_ref, ysgu_ref, yattn_ref, ymem_ref, wg0_ref, wg1_ref, wg2_ref,
                  gb_ref, wbs_ref, wba_ref, wbm_ref, wout_ref, o_ref, h_sc):
    j = pl.program_id(1)

    @pl.when(j == 0)
    def _():
        x = x_ref[...]
        h_sc[...] = _rms(x, nmix_ref[...]).astype(BF16)
        o_ref[...] = x

    h = h_sc[...]
    y = jax.nn.sigmoid(_dot(h, wg0_ref[...]) + gb_ref[0:1, :]) * _dot(ysgu_ref[...], wbs_ref[...])
    y += jax.nn.sigmoid(_dot(h, wg1_ref[...]) + gb_ref[1:2, :]) * _dot(yattn_ref[...], wba_ref[...])
    y += jax.nn.sigmoid(_dot(h, wg2_ref[...]) + gb_ref[2:3, :]) * _dot(ymem_ref[...], wbm_ref[...])
    o_ref[...] += _dot(y.astype(BF16), wout_ref[...])


def _merge(x2d, norm_mix, y_sgu, y_attn, y_mem, w_in, gate_b, w_br_sgu, w_br_attn, w_br_mem,
           w_out, tm, tn, layer):
    t, d = x2d.shape
    nj = d // tn
    gate0 = MIX_WIDTH // tn
    row = lambda i, j: (i, 0)
    col = lambda i, j: (0, j)
    lcol = lambda i, j: (layer, 0, j)
    return pl.pallas_call(
        _merge_kernel,
        grid=(t // tm, nj),
        in_specs=[
            pl.BlockSpec((tm, d), row),
            pl.BlockSpec((1, d), lambda i, j: (0, 0)),
            pl.BlockSpec((tm, SGU_WIDTH), row),
            pl.BlockSpec((tm, Q_WIDTH), row),
            pl.BlockSpec((tm, MEM_WIDTH), row),
            pl.BlockSpec((None, d, tn), lambda i, j: (layer, 0, gate0 + j)),
            pl.BlockSpec((None, d, tn), lambda i, j: (layer, 0, gate0 + nj + j)),
            pl.BlockSpec((None, d, tn), lambda i, j: (layer, 0, gate0 + 2 * nj + j)),
            pl.BlockSpec((N_BRANCH, tn), col),
            pl.BlockSpec((None, SGU_WIDTH, tn), lcol),
            pl.BlockSpec((None, Q_WIDTH, tn), lcol),
            pl.BlockSpec((None, MEM_WIDTH, tn), lcol),
            pl.BlockSpec((None, tn, d), lambda i, j: (layer, j, 0)),
        ],
        out_specs=pl.BlockSpec((tm, d), row),
        out_shape=jax.ShapeDtypeStruct((t, d), F32),
        scratch_shapes=[pltpu.VMEM((tm, d), BF16)],
        compiler_params=_params(("arbitrary", "arbitrary")),
        name="merge",
    )(x2d, norm_mix, y_sgu, y_attn, y_mem, w_in, w_in, w_in, gate_b, w_br_sgu, w_br_attn,
      w_br_mem, w_out)


def _ffn_kernel(x_ref, nffn_ref, wg_ref, wu_ref, wd_ref, o_ref, h_sc):
    j = pl.program_id(1)

    @pl.when(j == 0)
    def _():
        x = x_ref[...]
        h_sc[...] = _rms(x, nffn_ref[...]).astype(BF16)
        o_ref[...] = x

    h = h_sc[...]
    a = _dot(h, wg_ref[...])
    b = _dot(h, wu_ref[...])
    o_ref[...] += _dot((a * jax.nn.sigmoid(a) * b).astype(BF16), wd_ref[...])


def _ffn(x2d, norm_ffn, w_gate_up, w_down, tm, tf, layer):
    t, d = x2d.shape
    d_ff = w_down.shape[1]
    nj = d_ff // tf
    row = lambda i, j: (i, 0)
    return pl.pallas_call(
        _ffn_kernel,
        grid=(t // tm, nj),
        in_specs=[
            pl.BlockSpec((tm, d), row),
            pl.BlockSpec((1, d), lambda i, j: (0, 0)),
            pl.BlockSpec((None, d, tf), lambda i, j: (layer, 0, j)),
            pl.BlockSpec((None, d, tf), lambda i, j: (layer, 0, nj + j)),
            pl.BlockSpec((None, tf, d), lambda i, j: (layer, j, 0)),
        ],
        out_specs=pl.BlockSpec((tm, d), row),
        out_shape=jax.ShapeDtypeStruct((t, d), F32),
        scratch_shapes=[pltpu.VMEM((tm, d), BF16)],
        compiler_params=_params(("arbitrary", "arbitrary")),
        name="ffn",
    )(x2d, norm_ffn, w_gate_up, w_gate_up, w_down)


def _rope_tables(seq):
    quarter = HEAD_DIM // 4
    inv = ROPE_THETA ** (-jnp.arange(0, 2 * quarter, 2, dtype=F32) / (2 * quarter))
    pos = jnp.arange(seq, dtype=jnp.int32)
    ar = (pos // GRID_W).astype(F32)[:, None] * inv[None, :]
    ac = (pos % GRID_W).astype(F32)[:, None] * inv[None, :]
    cos_t = jnp.concatenate([jnp.cos(ar), jnp.cos(ar), jnp.cos(ac), jnp.cos(ac)], axis=-1)
    sin_t = jnp.concatenate([-jnp.sin(ar), jnp.sin(ar), -jnp.sin(ac), jnp.sin(ac)], axis=-1)
    return cos_t, sin_t


def _tile(n, want):
    if n <= want:
        return n
    for cand in range(want, 0, -128):
        if n % cand == 0:
            return cand
    return n


def kernel(x, mem, norm_mix, w_in, gate_b, sgu_norm, sgu_w, sgu_b, q_norm, k_norm, mem_norm,
           w_mem_kv, mq_norm, mk_norm, w_br_sgu, w_br_attn, w_br_mem, w_out, norm_ffn,
           w_gate_up, w_down):
    batch, seq, d = x.shape
    n_mem = mem.shape[1]
    depth = w_in.shape[0]
    assert seq % SGU_CHUNK == 0 and seq % GRID_W == 0

    tm = _tile(seq, 512)
    tq = _tile(seq, 512)
    tk = _tile(seq, 1024)
    tq_fast = _tile(seq, 2048)
    tk_fast = _tile(seq, 2048)
    tn = _tile(d, 512)
    tm_ffn = _tile(seq, 1024)
    tf = _tile(w_down.shape[1], 512)

    cos_t, sin_t = _rope_tables(seq)
    x2d = x.reshape(batch * seq, d)
    mem2d = mem.reshape(batch * n_mem, d)
    row = lambda a: a.reshape(1, -1)

    w_in_b, w_mem_kv_b, sgu_w_b = w_in.astype(BF16), w_mem_kv.astype(BF16), sgu_w.astype(BF16)
    w_br_sgu_b, w_br_attn_b, w_br_mem_b = (w.astype(BF16) for w in (w_br_sgu, w_br_attn, w_br_mem))
    w_out_b, w_gate_up_b, w_down_b = w_out.astype(BF16), w_gate_up.astype(BF16), w_down.astype(BF16)

    for l in range(depth):
        sgu_b_full = jnp.repeat(sgu_b[l].T, HEAD_DIM, axis=1)
        mk, mv = _memkv(mem2d, row(mem_norm[l]), w_mem_kv_b, row(mk_norm[l]), n_mem, l)
        y_sgu, q, k, v_ext, y_mem, kn2 = _inproj(
            x2d, row(norm_mix[l]), w_in_b, cos_t, sin_t, row(sgu_norm[l]), sgu_w_b, sgu_b_full,
            row(q_norm[l]), row(k_norm[l]), row(mq_norm[l]), mk, mv, seq, n_mem, tm, l)
        y_attn = _attention(q, k, v_ext, kn2, batch, seq, tm, tq, tk, tq_fast, tk_fast)
        x2d = _merge(x2d, row(norm_mix[l]), y_sgu, y_attn, y_mem, w_in_b, gate_b[l], w_br_sgu_b,
                     w_br_attn_b, w_br_mem_b, w_out_b, tm, tn, l)
        x2d = _ffn(x2d, row(norm_ffn[l]), w_gate_up_b, w_down_b, tm_ffn, tf, l)
    return x2d.reshape(batch, seq, d)
```

```python
import math

import jax
import jax.numpy as jnp
from jax import lax
from jax.experimental import pallas as pl
from jax.experimental.pallas import tpu as pltpu

HEAD_DIM = 128
N_Q_HEADS = 8
N_KV_HEADS = 2
GQA_GROUP = N_Q_HEADS // N_KV_HEADS
N_MEM_HEADS = 4
SGU_GROUPS = 4
SGU_CHUNK = 128
GRID_W = 64
ROPE_THETA = 10000.0
N_BRANCH = 3
EPS = 1e-6

SGU_WIDTH = SGU_GROUPS * HEAD_DIM
Q_WIDTH = N_Q_HEADS * HEAD_DIM
KV_WIDTH = N_KV_HEADS * HEAD_DIM
MEM_WIDTH = N_MEM_HEADS * HEAD_DIM
MIX_WIDTH = 2 * SGU_WIDTH + Q_WIDTH + 2 * KV_WIDTH + MEM_WIDTH
OFF_Q = 2 * SGU_WIDTH
OFF_K = OFF_Q + Q_WIDTH
OFF_V = OFF_K + KV_WIDTH
OFF_MQ = OFF_V + KV_WIDTH

LOG2E = math.log2(math.e)
ATTN_SCALE = HEAD_DIM ** -0.5

V7X_VMEM_LIMIT_BYTES = 56 * 1024 * 1024
SUBLANES = 8

STAB_SLACK = 60.0
ROW_SUM_FLOOR = 2.0 ** -80
KEY_NORM_MARGIN = 1.01

BF16 = jnp.bfloat16
F32 = jnp.float32


def _dot(a, b):
    return jnp.dot(a, b, preferred_element_type=F32)


def _dot_nt(a, b):
    return lax.dot_general(a, b, (((1,), (1,)), ((), ())), preferred_element_type=F32)


def _rms(z, gain):
    return z * lax.rsqrt(jnp.mean(z * z, axis=-1, keepdims=True) + EPS) * gain


def _gelu(z):
    return 0.5 * z * (1.0 + lax.erf(z * math.sqrt(0.5)))


def _rope(z, cos, sin_signed, first_of_pair):
    partner = jnp.where(first_of_pair, pltpu.roll(z, 96, 1), pltpu.roll(z, 32, 1))
    return z * cos + partner * sin_signed


def _params(sem):
    return pltpu.CompilerParams(dimension_semantics=sem, vmem_limit_bytes=V7X_VMEM_LIMIT_BYTES)


def _memkv_kernel(mem_ref, mnorm_ref, w_ref, mkn_ref, mk_ref, mv_ref):
    hm = _rms(mem_ref[...], mnorm_ref[...]).astype(BF16)
    kv = _dot(hm, w_ref[...])
    for hh in range(N_MEM_HEADS):
        sl = slice(hh * HEAD_DIM, (hh + 1) * HEAD_DIM)
        mk_ref[:, sl] = _rms(kv[:, sl], mkn_ref[...]).astype(BF16)
    mv_ref[...] = kv[:, MEM_WIDTH:].astype(BF16)


def _memkv(mem2d, mem_norm, w_mem_kv, mk_norm, n_mem, layer):
    rows, d = mem2d.shape
    return pl.pallas_call(
        _memkv_kernel,
        grid=(rows // n_mem,),
        in_specs=[
            pl.BlockSpec((n_mem, d), lambda i: (i, 0)),
            pl.BlockSpec((1, d), lambda i: (0, 0)),
            pl.BlockSpec((None, d, 2 * MEM_WIDTH), lambda i: (layer, 0, 0)),
            pl.BlockSpec((1, HEAD_DIM), lambda i: (0, 0)),
        ],
        out_specs=[
            pl.BlockSpec((n_mem, MEM_WIDTH), lambda i: (i, 0)),
            pl.BlockSpec((n_mem, MEM_WIDTH), lambda i: (i, 0)),
        ],
        out_shape=[jax.ShapeDtypeStruct((rows, MEM_WIDTH), BF16)] * 2,
        compiler_params=_params(("arbitrary",)),
        name="memkv",
    )(mem2d, mem_norm, w_mem_kv, mk_norm)


def _inproj_kernel(x_ref, nmix_ref, w_ref, cos_ref, sin_ref, sgun_ref, sguw_ref, sgub_ref,
                   qn_ref, kn_ref, mqn_ref, mk_ref, mv_ref,
                   ysgu_ref, q_ref, k_ref, v_ref, ymem_ref, kn2_ref, h_ref, za_sc, zb_sc):
    i = pl.program_id(0)
    rest = (cos_ref, sin_ref, sgun_ref, sguw_ref, sgub_ref, qn_ref, kn_ref, mqn_ref, mk_ref, mv_ref,
            ysgu_ref, q_ref, k_ref, v_ref, ymem_ref, kn2_ref)

    @pl.when(i == 0)
    def _():
        zb_sc[...] = jnp.zeros(zb_sc.shape, F32)

    def step(z_write, z_read):
        h = _rms(x_ref[...], nmix_ref[...]).astype(BF16)
        h_ref[...] = h
        z_write[...] = _dot(h, w_ref[...])
        _inproj_finish(z_read, *rest)

    @pl.when(i % 2 == 0)
    def _():
        step(za_sc, zb_sc)

    @pl.when(i % 2 == 1)
    def _():
        step(zb_sc, za_sc)


def _inproj_finish(z, cos_ref, sin_ref, sgun_ref, sguw_ref, sgub_ref, qn_ref, kn_ref, mqn_ref,
                   mk_ref, mv_ref, ysgu_ref, q_ref, k_ref, v_ref, ymem_ref, kn2_ref):
    tm = z.shape[0]

    zs = _gelu(z[:, 0:OFF_Q])
    for g in range(SGU_GROUPS):
        sl = slice(g * HEAD_DIM, (g + 1) * HEAD_DIM)
        u_g = zs[:, sl]
        vs_g = _rms(zs[:, SGU_WIDTH + g * HEAD_DIM:SGU_WIDTH + (g + 1) * HEAD_DIM],
                    sgun_ref[...]).astype(BF16)
        w_g = sguw_ref[g]
        for c in range(tm // SGU_CHUNK):
            rows = slice(c * SGU_CHUNK, (c + 1) * SGU_CHUNK)
            sp = _dot(w_g, vs_g[rows, :]) + sgub_ref[:, sl]
            ysgu_ref[rows, sl] = (u_g[rows, :] * sp).astype(BF16)

    cos = cos_ref[...]
    sin_signed = sin_ref[...]
    lane = lax.broadcasted_iota(jnp.int32, (tm, HEAD_DIM), 1)
    first_of_pair = (lane & 32) == 0
    zq = z[:, OFF_Q:OFF_K]
    for hh in range(N_Q_HEADS):
        sl = slice(hh * HEAD_DIM, (hh + 1) * HEAD_DIM)
        qh = _rope(_rms(zq[:, sl], qn_ref[...]), cos, sin_signed, first_of_pair)
        q_ref[:, sl] = (qh * (ATTN_SCALE * LOG2E)).astype(BF16)
    zkv = z[:, OFF_K:OFF_MQ]
    for hh in range(N_KV_HEADS):
        sl = slice(hh * HEAD_DIM, (hh + 1) * HEAD_DIM)
        kh = _rope(_rms(zkv[:, sl], kn_ref[...]), cos, sin_signed, first_of_pair)
        k_ref[:, sl] = kh.astype(BF16)
        kn2 = jnp.max(jnp.sum(kh * kh, axis=-1, keepdims=True), axis=0, keepdims=True)
        kn2_ref[:, sl] = jnp.broadcast_to(kn2, (kn2_ref.shape[0], HEAD_DIM))
        v_ref[:, 2 * hh * HEAD_DIM:(2 * hh + 1) * HEAD_DIM] = zkv[:, KV_WIDTH + hh * HEAD_DIM:
                                                                   KV_WIDTH + (hh + 1) * HEAD_DIM].astype(BF16)
        v_ref[:, (2 * hh + 1) * HEAD_DIM:(2 * hh + 2) * HEAD_DIM] = jnp.ones((tm, HEAD_DIM), BF16)

    zmq = z[:, OFF_MQ:MIX_WIDTH]
    for hh in range(N_MEM_HEADS):
        sl = slice(hh * HEAD_DIM, (hh + 1) * HEAD_DIM)
        mq = (_rms(zmq[:, sl], mqn_ref[...]) * (ATTN_SCALE * LOG2E)).astype(BF16)
        s = _dot_nt(mq, mk_ref[:, sl])
        p = jnp.exp2(s - jnp.max(s, axis=-1, keepdims=True))
        o = _dot(p.astype(BF16), mv_ref[:, sl])
        ymem_ref[:, sl] = (o / jnp.sum(p, axis=-1, keepdims=True)).astype(BF16)


def _inproj(x2d, norm_mix, w_in, cos_t, sin_t, sgu_norm, sgu_w, sgu_b_full, q_norm, k_norm,
            mq_norm, mk, mv, seq, n_mem, tm, layer):
    t, d = x2d.shape
    tiles_per_seq = seq // tm
    n_tiles = t // tm
    const = lambda i: (0, 0)
    prev = lambda i: jnp.maximum(i - 1, 0)
    row = lambda i: (prev(i), 0)
    cur = lambda i: (jnp.minimum(i, n_tiles - 1), 0)
    return pl.pallas_call(
        _inproj_kernel,
        grid=(n_tiles + 1,),
        in_specs=[
            pl.BlockSpec((tm, d), cur),
            pl.BlockSpec((1, d), const),
            pl.BlockSpec((None, d, MIX_WIDTH), lambda i: (layer, 0, 0), pipeline_mode=pl.Buffered(1)),
            pl.BlockSpec((tm, HEAD_DIM), lambda i: (prev(i) % tiles_per_seq, 0)),
            pl.BlockSpec((tm, HEAD_DIM), lambda i: (prev(i) % tiles_per_seq, 0)),
            pl.BlockSpec((1, HEAD_DIM), const),
            pl.BlockSpec((None, SGU_GROUPS, SGU_CHUNK, SGU_CHUNK), lambda i: (layer, 0, 0, 0)),
            pl.BlockSpec((SGU_CHUNK, SGU_WIDTH), const),
            pl.BlockSpec((1, HEAD_DIM), const),
            pl.BlockSpec((1, HEAD_DIM), const),
            pl.BlockSpec((1, HEAD_DIM), const),
            pl.BlockSpec((n_mem, MEM_WIDTH), lambda i: (prev(i) // tiles_per_seq, 0)),
            pl.BlockSpec((n_mem, MEM_WIDTH), lambda i: (prev(i) // tiles_per_seq, 0)),
        ],
        out_specs=[
            pl.BlockSpec((tm, SGU_WIDTH), row),
            pl.BlockSpec((tm, Q_WIDTH), row),
            pl.BlockSpec((tm, KV_WIDTH), row),
            pl.BlockSpec((tm, 2 * KV_WIDTH), row),
            pl.BlockSpec((tm, MEM_WIDTH), row),
            pl.BlockSpec((SUBLANES, KV_WIDTH), row),
            pl.BlockSpec((tm, d), cur),
        ],
        out_shape=[
            jax.ShapeDtypeStruct((t, SGU_WIDTH), BF16),
            jax.ShapeDtypeStruct((t, Q_WIDTH), BF16),
            jax.ShapeDtypeStruct((t, KV_WIDTH), BF16),
            jax.ShapeDtypeStruct((t, 2 * KV_WIDTH), BF16),
            jax.ShapeDtypeStruct((t, MEM_WIDTH), BF16),
            jax.ShapeDtypeStruct((t // tm * SUBLANES, KV_WIDTH), F32),
            jax.ShapeDtypeStruct((t, d), BF16),
        ],
        scratch_shapes=[pltpu.VMEM((tm, MIX_WIDTH), F32), pltpu.VMEM((tm, MIX_WIDTH), F32)],
        compiler_params=_params(("arbitrary",)),
        name="inproj",
    )(x2d, norm_mix, w_in, cos_t, sin_t, sgu_norm, sgu_w, sgu_b_full, q_norm, k_norm, mq_norm, mk, mv)


def _flash_kernel(q_ref, k_ref, v_ref, o_ref, m_sc, l_sc, acc_sc):
    ki = pl.program_id(3)

    @pl.when(ki == 0)
    def _():
        m_sc[...] = jnp.full(m_sc.shape, -jnp.inf, F32)
        l_sc[...] = jnp.zeros(l_sc.shape, F32)
        acc_sc[...] = jnp.zeros(acc_sc.shape, F32)

    k = k_ref[...]
    v = v_ref[...]
    for hh in range(GQA_GROUP):
        sl = slice(hh * HEAD_DIM, (hh + 1) * HEAD_DIM)
        s = _dot_nt(q_ref[:, sl], k)
        m_prev = m_sc[hh]
        m_new = jnp.maximum(m_prev, jnp.max(s, axis=-1, keepdims=True))
        alpha = jnp.exp2(m_prev - m_new)
        p = jnp.exp2(s - m_new[:, 0:1])
        l_sc[hh] = alpha * l_sc[hh] + jnp.sum(p, axis=-1, keepdims=True)
        acc_sc[hh] = alpha * acc_sc[hh] + _dot(p.astype(BF16), v)
        m_sc[hh] = m_new

    @pl.when(ki == pl.num_programs(3) - 1)
    def _():
        for hh in range(GQA_GROUP):
            sl = slice(hh * HEAD_DIM, (hh + 1) * HEAD_DIM)
            o_ref[:, sl] = (acc_sc[hh] / l_sc[hh]).astype(BF16)


def _flash_fast_kernel(kmax_ref, q_ref, k_ref, v_ref, o_ref, lmin_ref, c_sc, acc_sc):
    ki = pl.program_id(3)
    tq = q_ref.shape[0]
    tk = k_ref.shape[0]

    @pl.when(ki == 0)
    def _():
        kmax = kmax_ref[pl.program_id(0), pl.program_id(1)]
        for hh in range(GQA_GROUP):
            qf = q_ref[:, hh * HEAD_DIM:(hh + 1) * HEAD_DIM].astype(F32)
            qn = jnp.sqrt(jnp.sum(qf * qf, axis=-1, keepdims=True))
            c_sc[hh] = jnp.broadcast_to(qn * kmax - STAB_SLACK, (tq, HEAD_DIM))
        acc_sc[...] = jnp.zeros(acc_sc.shape, F32)

    k = k_ref[...]
    v = v_ref[...]
    for hh in range(GQA_GROUP):
        s = _dot_nt(q_ref[:, hh * HEAD_DIM:(hh + 1) * HEAD_DIM], k)
        p = jnp.exp2(s - jnp.tile(c_sc[hh], (1, tk // HEAD_DIM)))
        acc_sc[hh] += _dot(p.astype(BF16), v)

    @pl.when(ki == pl.num_programs(3) - 1)
    def _():
        lmin = None
        for hh in range(GQA_GROUP):
            acc = acc_sc[hh]
            l = acc[:, HEAD_DIM:]
            o_ref[:, hh * HEAD_DIM:(hh + 1) * HEAD_DIM] = (acc[:, :HEAD_DIM] / l).astype(BF16)
            lm = jnp.min(l, axis=0, keepdims=True)
            lmin = lm if lmin is None else jnp.minimum(lmin, lm)
        lmin_ref[...] = jnp.broadcast_to(lmin, lmin_ref.shape)


def _flash_fast(kmax, q, k, v_ext, batch, seq, tq, tk):
    t = q.shape[0]
    nq = seq // tq
    nk = seq // tk
    group_w = GQA_GROUP * HEAD_DIM
    return pl.pallas_call(
        _flash_fast_kernel,
        grid=(batch, N_KV_HEADS, nq, nk),
        in_specs=[
            pl.BlockSpec(memory_space=pltpu.SMEM),
            pl.BlockSpec((tq, group_w), lambda b, g, qi, ki: (b * nq + qi, g)),
            pl.BlockSpec((tk, HEAD_DIM), lambda b, g, qi, ki: (b * nk + ki, g)),
            pl.BlockSpec((tk, 2 * HEAD_DIM), lambda b, g, qi, ki: (b * nk + ki, g)),
        ],
        out_specs=[
            pl.BlockSpec((tq, group_w), lambda b, g, qi, ki: (b * nq + qi, g)),
            pl.BlockSpec((SUBLANES, HEAD_DIM), lambda b, g, qi, ki: (b * nq + qi, g)),
        ],
        out_shape=[
            jax.ShapeDtypeStruct((t, Q_WIDTH), BF16),
            jax.ShapeDtypeStruct((batch * nq * SUBLANES, N_KV_HEADS * HEAD_DIM), F32),
        ],
        scratch_shapes=[
            pltpu.VMEM((GQA_GROUP, tq, HEAD_DIM), F32),
            pltpu.VMEM((GQA_GROUP, tq, 2 * HEAD_DIM), F32),
        ],
        compiler_params=_params(("arbitrary", "arbitrary", "arbitrary", "arbitrary")),
        name="flash_fast",
    )(kmax, q, k, v_ext)


def _flash(q, k, v_ext, batch, seq, tq, tk):
    t = q.shape[0]
    nq = seq // tq
    nk = seq // tk
    group_w = GQA_GROUP * HEAD_DIM
    return pl.pallas_call(
        _flash_kernel,
        grid=(batch, N_KV_HEADS, nq, nk),
        in_specs=[
            pl.BlockSpec((tq, group_w), lambda b, g, qi, ki: (b * nq + qi, g)),
            pl.BlockSpec((tk, HEAD_DIM), lambda b, g, qi, ki: (b * nk + ki, g)),
            pl.BlockSpec((tk, HEAD_DIM), lambda b, g, qi, ki: (b * nk + ki, 2 * g)),
        ],
        out_specs=pl.BlockSpec((tq, group_w), lambda b, g, qi, ki: (b * nq + qi, g)),
        out_shape=jax.ShapeDtypeStruct((t, Q_WIDTH), BF16),
        scratch_shapes=[
            pltpu.VMEM((GQA_GROUP, tq, HEAD_DIM), F32),
            pltpu.VMEM((GQA_GROUP, tq, HEAD_DIM), F32),
            pltpu.VMEM((GQA_GROUP, tq, HEAD_DIM), F32),
        ],
        compiler_params=_params(("arbitrary", "arbitrary", "arbitrary", "arbitrary")),
        name="flash",
    )(q, k, v_ext)


def _attention(q, k, v_ext, kn2, batch, seq, tm, tq, tk, tq_fast, tk_fast):
    kmax = jnp.sqrt(jnp.max(
        kn2.reshape(batch, seq // tm, SUBLANES, N_KV_HEADS, HEAD_DIM)[:, :, 0, :, 0], axis=1))
    y_fast, lmin = _flash_fast(kmax * KEY_NORM_MARGIN, q, k, v_ext, batch, seq, tq_fast, tk_fast)
    trusted = jnp.all(lmin >= ROW_SUM_FLOOR)
    return lax.cond(trusted, lambda: y_fast, lambda: _flash(q, k, v_ext, batch, seq, tq, tk))


def _gatemix_kernel(h_ref, ysgu_ref, yattn_ref, ymem_ref, wg0_ref, wg1_ref, wg2_ref, gb_ref,
                    wbs_ref, wba_ref, wbm_ref, y_ref):
    h = h_ref[...]
    y = jax.nn.sigmoid(_dot(h, wg0_ref[...]) + gb_ref[0:1, :]) * _dot(ysgu_ref[...], wbs_ref[...])
    y += jax.nn.sigmoid(_dot(h, wg1_ref[...]) + gb_ref[1:2, :]) * _dot(yattn_ref[...], wba_ref[...])
    y += jax.nn.sigmoid(_dot(h, wg2_ref[...]) + gb_ref[2:3, :]) * _dot(ymem_ref[...], wbm_ref[...])
    y_ref[...] = y.astype(BF16)


def _gatemix(h, y_sgu, y_attn, y_mem, w_in, gate_b, w_br_sgu, w_br_attn, w_br_mem, tm, tn, layer):
    t, d = h.shape
    nj = d // tn
    gate0 = MIX_WIDTH // tn
    row = lambda i, j: (i, 0)
    lcol = lambda i, j: (layer, 0, j)
    return pl.pallas_call(
        _gatemix_kernel,
        grid=(t // tm, nj),
        in_specs=[
            pl.BlockSpec((tm, d), row),
            pl.BlockSpec((tm, SGU_WIDTH), row),
            pl.BlockSpec((tm, Q_WIDTH), row),
            pl.BlockSpec((tm, MEM_WIDTH), row),
            pl.BlockSpec((None, d, tn), lambda i, j: (layer, 0, gate0 + j)),
            pl.BlockSpec((None, d, tn), lambda i, j: (layer, 0, gate0 + nj + j)),
            pl.BlockSpec((None, d, tn), lambda i, j: (layer, 0, gate0 + 2 * nj + j)),
            pl.BlockSpec((N_BRANCH, tn), lambda i, j: (0, j)),
            pl.BlockSpec((None, SGU_WIDTH, tn), lcol),
            pl.BlockSpec((None, Q_WIDTH, tn), lcol),
            pl.BlockSpec((None, MEM_WIDTH, tn), lcol),
        ],
        out_specs=pl.BlockSpec((tm, tn), lambda i, j: (i, j)),
        out_shape=jax.ShapeDtypeStruct((t, d), BF16),
        compiler_params=_params(("arbitrary", "arbitrary")),
        name="gatemix",
    )(h, y_sgu, y_attn, y_mem, w_in, w_in, w_in, gate_b, w_br_sgu, w_br_attn, w_br_mem)


def _outproj_kernel(x_ref, y_ref, w_ref, o_ref):
    o_ref[...] = x_ref[...] + _dot(y_ref[...], w_ref[...])


def _outproj(x2d, y, w_out, tm, layer):
    t, d = x2d.shape
    row = lambda i: (i, 0)
    return pl.pallas_call(
        _outproj_kernel,
        grid=(t // tm,),
        in_specs=[
            pl.BlockSpec((tm, d), row),
            pl.BlockSpec((tm, d), row),
            pl.BlockSpec((None, d, d), lambda i: (layer, 0, 0), pipeline_mode=pl.Buffered(1)),
        ],
        out_specs=pl.BlockSpec((tm, d), row),
        out_shape=jax.ShapeDtypeStruct((t, d), F32),
        compiler_params=_params(("arbitrary",)),
        name="outproj",
    )(x2d, y, w_out)


def _ffn_kernel(x_ref, nffn_ref, wg_ref, wu_ref, wd_ref, o_ref, h_sc):
    j = pl.program_id(1)

    @pl.when(j == 0)
    def _():
        x = x_ref[...]
        h_sc[...] = _rms(x, nffn_ref[...]).astype(BF16)
        o_ref[...] = x

    h = h_sc[...]
    a = _dot(h, wg_ref[...])
    b = _dot(h, wu_ref[...])
    o_ref[...] += _dot((a * jax.nn.sigmoid(a) * b).astype(BF16), wd_ref[...])


def _ffn(x2d, norm_ffn, w_gate_up, w_down, tm, tf, layer):
    t, d = x2d.shape
    d_ff = w_down.shape[1]
    nj = d_ff // tf
    row = lambda i, j: (i, 0)
    return pl.pallas_call(
        _ffn_kernel,
        grid=(t // tm, nj),
        in_specs=[
            pl.BlockSpec((tm, d), row),
            pl.BlockSpec((1, d), lambda i, j: (0, 0)),
            pl.BlockSpec((None, d, tf), lambda i, j: (layer, 0, j)),
            pl.BlockSpec((None, d, tf), lambda i, j: (layer, 0, nj + j)),
            pl.BlockSpec((None, tf, d), lambda i, j: (layer, j, 0)),
        ],
        out_specs=pl.BlockSpec((tm, d), row),
        out_shape=jax.ShapeDtypeStruct((t, d), F32),
        scratch_shapes=[pltpu.VMEM((tm, d), BF16)],
        compiler_params=_params(("arbitrary", "arbitrary")),
        name="ffn",
    )(x2d, norm_ffn, w_gate_up, w_gate_up, w_down)


def _rope_tables(seq):
    quarter = HEAD_DIM // 4
    inv = ROPE_THETA ** (-jnp.arange(0, 2 * quarter, 2, dtype=F32) / (2 * quarter))
    pos = jnp.arange(seq, dtype=jnp.int32)
    ar = (pos // GRID_W).astype(F32)[:, None] * inv[None, :]
    ac = (pos % GRID_W).astype(F32)[:, None] * inv[None, :]
    cos_t = jnp.concatenate([jnp.cos(ar), jnp.cos(ar), jnp.cos(ac), jnp.cos(ac)], axis=-1)
    sin_t = jnp.concatenate([-jnp.sin(ar), jnp.sin(ar), -jnp.sin(ac), jnp.sin(ac)], axis=-1)
    return cos_t, sin_t


def _tile(n, want):
    if n <= want:
        return n
    for cand in range(want, 0, -128):
        if n % cand == 0:
            return cand
    return n


def kernel(x, mem, norm_mix, w_in, gate_b, sgu_norm, sgu_w, sgu_b, q_norm, k_norm, mem_norm,
           w_mem_kv, mq_norm, mk_norm, w_br_sgu, w_br_attn, w_br_mem, w_out, norm_ffn,
           w_gate_up, w_down):
    batch, seq, d = x.shape
    n_mem = mem.shape[1]
    depth = w_in.shape[0]
    assert seq % SGU_CHUNK == 0 and seq % GRID_W == 0

    tm = _tile(seq, 512)
    tq = _tile(seq, 512)
    tk = _tile(seq, 1024)
    tq_fast = _tile(seq, 2048)
    tk_fast = _tile(seq, 2048)
    tn = _tile(d, 512)
    tm_big = _tile(seq, 1024)
    tf = _tile(w_down.shape[1], 512)

    cos_t, sin_t = _rope_tables(seq)
    x2d = x.reshape(batch * seq, d)
    mem2d = mem.reshape(batch * n_mem, d)
    row = lambda a: a.reshape(1, -1)

    w_in_b, w_mem_kv_b, sgu_w_b = w_in.astype(BF16), w_mem_kv.astype(BF16), sgu_w.astype(BF16)
    w_br_sgu_b, w_br_attn_b, w_br_mem_b = (w.astype(BF16) for w in (w_br_sgu, w_br_attn, w_br_mem))
    w_out_b, w_gate_up_b, w_down_b = w_out.astype(BF16), w_gate_up.astype(BF16), w_down.astype(BF16)

    for l in range(depth):
        sgu_b_full = jnp.repeat(sgu_b[l].T, HEAD_DIM, axis=1)
        mk, mv = _memkv(mem2d, row(mem_norm[l]), w_mem_kv_b, row(mk_norm[l]), n_mem, l)
        y_sgu, q, k, v_ext, y_mem, kn2, h = _inproj(
            x2d, row(norm_mix[l]), w_in_b, cos_t, sin_t, row(sgu_norm[l]), sgu_w_b, sgu_b_full,
            row(q_norm[l]), row(k_norm[l]), row(mq_norm[l]), mk, mv, seq, n_mem, tm, l)
        y_attn = _attention(q, k, v_ext, kn2, batch, seq, tm, tq, tk, tq_fast, tk_fast)
        y = _gatemix(h, y_sgu, y_attn, y_mem, w_in_b, gate_b[l], w_br_sgu_b, w_br_attn_b, w_br_mem_b,
                     tm_big, tn, l)
        x2d = _outproj(x2d, y, w_out_b, tm_big, l)
        x2d = _ffn(x2d, row(norm_ffn[l]), w_gate_up_b, w_down_b, tm_big, tf, l)
    return x2d.reshape(batch, seq, d)
```

```python
import math

import jax
import jax.numpy as jnp
from jax import lax
from jax.experimental import pallas as pl
from jax.experimental.pallas import tpu as pltpu

HEAD_DIM = 128
N_Q_HEADS = 8
N_KV_HEADS = 2
GQA_GROUP = N_Q_HEADS // N_KV_HEADS
N_MEM_HEADS = 4
SGU_GROUPS = 4
SGU_CHUNK = 128
GRID_W = 64
ROPE_THETA = 10000.0
N_BRANCH = 3
EPS = 1e-6

SGU_WIDTH = SGU_GROUPS * HEAD_DIM
Q_WIDTH = N_Q_HEADS * HEAD_DIM
KV_WIDTH = N_KV_HEADS * HEAD_DIM
MEM_WIDTH = N_MEM_HEADS * HEAD_DIM
MIX_WIDTH = 2 * SGU_WIDTH + Q_WIDTH + 2 * KV_WIDTH + MEM_WIDTH
OFF_Q = 2 * SGU_WIDTH
OFF_K = OFF_Q + Q_WIDTH
OFF_V = OFF_K + KV_WIDTH
OFF_MQ = OFF_V + KV_WIDTH

LOG2E = math.log2(math.e)
ATTN_SCALE = HEAD_DIM ** -0.5

V7X_VMEM_LIMIT_BYTES = 56 * 1024 * 1024
SUBLANES = 8

STAB_SLACK = 60.0
ROW_SUM_FLOOR = 2.0 ** -80
KEY_NORM_MARGIN = 1.01

BF16 = jnp.bfloat16
F32 = jnp.float32


def _dot(a, b):
    return jnp.dot(a, b, preferred_element_type=F32)


def _dot_nt(a, b):
    return lax.dot_general(a, b, (((1,), (1,)), ((), ())), preferred_element_type=F32)


def _rms(z, gain):
    return z * lax.rsqrt(jnp.mean(z * z, axis=-1, keepdims=True) + EPS) * gain


def _gelu(z):
    return 0.5 * z * (1.0 + lax.erf(z * math.sqrt(0.5)))


def _rope(z, cos, sin_signed, first_of_pair):
    partner = jnp.where(first_of_pair, pltpu.roll(z, 96, 1), pltpu.roll(z, 32, 1))
    return z * cos + partner * sin_signed


def _params(sem):
    return pltpu.CompilerParams(dimension_semantics=sem, vmem_limit_bytes=V7X_VMEM_LIMIT_BYTES)


def _memkv_kernel(mem_ref, mnorm_ref, w_ref, mkn_ref, mk_ref, mv_ref):
    hm = _rms(mem_ref[...], mnorm_ref[...]).astype(BF16)
    kv = _dot(hm, w_ref[...])
    for hh in range(N_MEM_HEADS):
        sl = slice(hh * HEAD_DIM, (hh + 1) * HEAD_DIM)
        mk_ref[:, sl] = _rms(kv[:, sl], mkn_ref[...]).astype(BF16)
    mv_ref[...] = kv[:, MEM_WIDTH:].astype(BF16)


def _memkv(mem2d, mem_norm, w_mem_kv, mk_norm, n_mem, layer):
    rows, d = mem2d.shape
    return pl.pallas_call(
        _memkv_kernel,
        grid=(rows // n_mem,),
        in_specs=[
            pl.BlockSpec((n_mem, d), lambda i: (i, 0)),
            pl.BlockSpec((1, d), lambda i: (0, 0)),
            pl.BlockSpec((None, d, 2 * MEM_WIDTH), lambda i: (layer, 0, 0)),
            pl.BlockSpec((1, HEAD_DIM), lambda i: (0, 0)),
        ],
        out_specs=[
            pl.BlockSpec((n_mem, MEM_WIDTH), lambda i: (i, 0)),
            pl.BlockSpec((n_mem, MEM_WIDTH), lambda i: (i, 0)),
        ],
        out_shape=[jax.ShapeDtypeStruct((rows, MEM_WIDTH), BF16)] * 2,
        compiler_params=_params(("arbitrary",)),
        name="memkv",
    )(mem2d, mem_norm, w_mem_kv, mk_norm)


def _inproj_kernel(x_ref, nmix_ref, w_ref, cos_ref, sin_ref, sgun_ref, sguw_ref, sgub_ref,
                   qn_ref, kn_ref, mqn_ref, mk_ref, mv_ref,
                   ysgu_ref, q_ref, k_ref, v_ref, ymem_ref, kn2_ref, h_ref, za_sc, zb_sc):
    i = pl.program_id(0)
    rest = (cos_ref, sin_ref, sgun_ref, sguw_ref, sgub_ref, qn_ref, kn_ref, mqn_ref, mk_ref, mv_ref,
            ysgu_ref, q_ref, k_ref, v_ref, ymem_ref, kn2_ref)

    @pl.when(i == 0)
    def _():
        zb_sc[...] = jnp.zeros(zb_sc.shape, F32)

    def step(z_write, z_read):
        h = _rms(x_ref[...], nmix_ref[...]).astype(BF16)
        h_ref[...] = h
        z_write[...] = _dot(h, w_ref[...])
        _inproj_finish(z_read, *rest)

    @pl.when(i % 2 == 0)
    def _():
        step(za_sc, zb_sc)

    @pl.when(i % 2 == 1)
    def _():
        step(zb_sc, za_sc)


def _inproj_finish(z, cos_ref, sin_ref, sgun_ref, sguw_ref, sgub_ref, qn_ref, kn_ref, mqn_ref,
                   mk_ref, mv_ref, ysgu_ref, q_ref, k_ref, v_ref, ymem_ref, kn2_ref):
    tm = z.shape[0]

    zs = _gelu(z[:, 0:OFF_Q])
    for g in range(SGU_GROUPS):
        sl = slice(g * HEAD_DIM, (g + 1) * HEAD_DIM)
        u_g = zs[:, sl]
        vs_g = _rms(zs[:, SGU_WIDTH + g * HEAD_DIM:SGU_WIDTH + (g + 1) * HEAD_DIM],
                    sgun_ref[...]).astype(BF16)
        w_g = sguw_ref[g]
        for c in range(tm // SGU_CHUNK):
            rows = slice(c * SGU_CHUNK, (c + 1) * SGU_CHUNK)
            sp = _dot(w_g, vs_g[rows, :]) + sgub_ref[:, sl]
            ysgu_ref[rows, sl] = (u_g[rows, :] * sp).astype(BF16)

    cos = cos_ref[...]
    sin_signed = sin_ref[...]
    lane = lax.broadcasted_iota(jnp.int32, (tm, HEAD_DIM), 1)
    first_of_pair = (lane & 32) == 0
    zq = z[:, OFF_Q:OFF_K]
    for hh in range(N_Q_HEADS):
        sl = slice(hh * HEAD_DIM, (hh + 1) * HEAD_DIM)
        qh = _rope(_rms(zq[:, sl], qn_ref[...]), cos, sin_signed, first_of_pair)
        q_ref[:, sl] = (qh * (ATTN_SCALE * LOG2E)).astype(BF16)
    zkv = z[:, OFF_K:OFF_MQ]
    for hh in range(N_KV_HEADS):
        sl = slice(hh * HEAD_DIM, (hh + 1) * HEAD_DIM)
        kh = _rope(_rms(zkv[:, sl], kn_ref[...]), cos, sin_signed, first_of_pair)
        k_ref[:, sl] = kh.astype(BF16)
        kn2 = jnp.max(jnp.sum(kh * kh, axis=-1, keepdims=True), axis=0, keepdims=True)
        kn2_ref[:, sl] = jnp.broadcast_to(kn2, (kn2_ref.shape[0], HEAD_DIM))
        v_ref[:, 2 * hh * HEAD_DIM:(2 * hh + 1) * HEAD_DIM] = zkv[:, KV_WIDTH + hh * HEAD_DIM:
                                                                   KV_WIDTH + (hh + 1) * HEAD_DIM].astype(BF16)
        v_ref[:, (2 * hh + 1) * HEAD_DIM:(2 * hh + 2) * HEAD_DIM] = jnp.ones((tm, HEAD_DIM), BF16)

    zmq = z[:, OFF_MQ:MIX_WIDTH]
    for hh in range(N_MEM_HEADS):
        sl = slice(hh * HEAD_DIM, (hh + 1) * HEAD_DIM)
        mq = (_rms(zmq[:, sl], mqn_ref[...]) * (ATTN_SCALE * LOG2E)).astype(BF16)
        s = _dot_nt(mq, mk_ref[:, sl])
        p = jnp.exp2(s - jnp.max(s, axis=-1, keepdims=True))
        o = _dot(p.astype(BF16), mv_ref[:, sl])
        ymem_ref[:, sl] = (o / jnp.sum(p, axis=-1, keepdims=True)).astype(BF16)


def _inproj(x2d, norm_mix, w_in, cos_t, sin_t, sgu_norm, sgu_w, sgu_b_full, q_norm, k_norm,
            mq_norm, mk, mv, seq, n_mem, tm, layer):
    t, d = x2d.shape
    tiles_per_seq = seq // tm
    n_tiles = t // tm
    const = lambda i: (0, 0)
    prev = lambda i: jnp.maximum(i - 1, 0)
    row = lambda i: (prev(i), 0)
    cur = lambda i: (jnp.minimum(i, n_tiles - 1), 0)
    return pl.pallas_call(
        _inproj_kernel,
        grid=(n_tiles + 1,),
        in_specs=[
            pl.BlockSpec((tm, d), cur),
            pl.BlockSpec((1, d), const),
            pl.BlockSpec((None, d, MIX_WIDTH), lambda i: (layer, 0, 0), pipeline_mode=pl.Buffered(1)),
            pl.BlockSpec((tm, HEAD_DIM), lambda i: (prev(i) % tiles_per_seq, 0)),
            pl.BlockSpec((tm, HEAD_DIM), lambda i: (prev(i) % tiles_per_seq, 0)),
            pl.BlockSpec((1, HEAD_DIM), const),
            pl.BlockSpec((None, SGU_GROUPS, SGU_CHUNK, SGU_CHUNK), lambda i: (layer, 0, 0, 0)),
            pl.BlockSpec((SGU_CHUNK, SGU_WIDTH), const),
            pl.BlockSpec((1, HEAD_DIM), const),
            pl.BlockSpec((1, HEAD_DIM), const),
            pl.BlockSpec((1, HEAD_DIM), const),
            pl.BlockSpec((n_mem, MEM_WIDTH), lambda i: (prev(i) // tiles_per_seq, 0)),
            pl.BlockSpec((n_mem, MEM_WIDTH), lambda i: (prev(i) // tiles_per_seq, 0)),
        ],
        out_specs=[
            pl.BlockSpec((tm, SGU_WIDTH), row),
            pl.BlockSpec((tm, Q_WIDTH), row),
            pl.BlockSpec((tm, KV_WIDTH), row),
            pl.BlockSpec((tm, 2 * KV_WIDTH), row),
            pl.BlockSpec((tm, MEM_WIDTH), row),
            pl.BlockSpec((SUBLANES, KV_WIDTH), row),
            pl.BlockSpec((tm, d), cur),
        ],
        out_shape=[
            jax.ShapeDtypeStruct((t, SGU_WIDTH), BF16),
            jax.ShapeDtypeStruct((t, Q_WIDTH), BF16),
            jax.ShapeDtypeStruct((t, KV_WIDTH), BF16),
            jax.ShapeDtypeStruct((t, 2 * KV_WIDTH), BF16),
            jax.ShapeDtypeStruct((t, MEM_WIDTH), BF16),
            jax.ShapeDtypeStruct((t // tm * SUBLANES, KV_WIDTH), F32),
            jax.ShapeDtypeStruct((t, d), BF16),
        ],
        scratch_shapes=[pltpu.VMEM((tm, MIX_WIDTH), F32), pltpu.VMEM((tm, MIX_WIDTH), F32)],
        compiler_params=_params(("arbitrary",)),
        name="inproj",
    )(x2d, norm_mix, w_in, cos_t, sin_t, sgu_norm, sgu_w, sgu_b_full, q_norm, k_norm, mq_norm, mk, mv)


def _flash_kernel(q_ref, k_ref, v_ref, o_ref, m_sc, l_sc, acc_sc):
    ki = pl.program_id(3)

    @pl.when(ki == 0)
    def _():
        m_sc[...] = jnp.full(m_sc.shape, -jnp.inf, F32)
        l_sc[...] = jnp.zeros(l_sc.shape, F32)
        acc_sc[...] = jnp.zeros(acc_sc.shape, F32)

    k = k_ref[...]
    v = v_ref[...]
    for hh in range(GQA_GROUP):
        sl = slice(hh * HEAD_DIM, (hh + 1) * HEAD_DIM)
        s = _dot_nt(q_ref[:, sl], k)
        m_prev = m_sc[hh]
        m_new = jnp.maximum(m_prev, jnp.max(s, axis=-1, keepdims=True))
        alpha = jnp.exp2(m_prev - m_new)
        p = jnp.exp2(s - m_new[:, 0:1])
        l_sc[hh] = alpha * l_sc[hh] + jnp.sum(p, axis=-1, keepdims=True)
        acc_sc[hh] = alpha * acc_sc[hh] + _dot(p.astype(BF16), v)
        m_sc[hh] = m_new

    @pl.when(ki == pl.num_programs(3) - 1)
    def _():
        for hh in range(GQA_GROUP):
            sl = slice(hh * HEAD_DIM, (hh + 1) * HEAD_DIM)
            o_ref[:, sl] = (acc_sc[hh] / l_sc[hh]).astype(BF16)


def _flash_fast_kernel(kmax_ref, q_ref, k_ref, v_ref, o_ref, lmin_ref, c_sc, acc_sc):
    ki = pl.program_id(3)
    tq = q_ref.shape[0]
    tk = k_ref.shape[0]

    @pl.when(ki == 0)
    def _():
        kmax = kmax_ref[pl.program_id(0), pl.program_id(1)]
        for hh in range(GQA_GROUP):
            qf = q_ref[:, hh * HEAD_DIM:(hh + 1) * HEAD_DIM].astype(F32)
            qn = jnp.sqrt(jnp.sum(qf * qf, axis=-1, keepdims=True))
            c_sc[hh] = jnp.broadcast_to(qn * kmax - STAB_SLACK, (tq, HEAD_DIM))
        acc_sc[...] = jnp.zeros(acc_sc.shape, F32)

    k = k_ref[...]
    v = v_ref[...]
    for hh in range(GQA_GROUP):
        s = _dot_nt(q_ref[:, hh * HEAD_DIM:(hh + 1) * HEAD_DIM], k)
        p = jnp.exp2(s - jnp.tile(c_sc[hh], (1, tk // HEAD_DIM)))
        acc_sc[hh] += _dot(p.astype(BF16), v)

    @pl.when(ki == pl.num_programs(3) - 1)
    def _():
        lmin = None
        for hh in range(GQA_GROUP):
            acc = acc_sc[hh]
            l = acc[:, HEAD_DIM:]
            o_ref[:, hh * HEAD_DIM:(hh + 1) * HEAD_DIM] = (acc[:, :HEAD_DIM] / l).astype(BF16)
            lm = jnp.min(l, axis=0, keepdims=True)
            lmin = lm if lmin is None else jnp.minimum(lmin, lm)
        lmin_ref[...] = jnp.broadcast_to(lmin, lmin_ref.shape)


def _flash_fast(kmax, q, k, v_ext, batch, seq, tq, tk):
    t = q.shape[0]
    nq = seq // tq
    nk = seq // tk
    group_w = GQA_GROUP * HEAD_DIM
    return pl.pallas_call(
        _flash_fast_kernel,
        grid=(batch, N_KV_HEADS, nq, nk),
        in_specs=[
            pl.BlockSpec(memory_space=pltpu.SMEM),
            pl.BlockSpec((tq, group_w), lambda b, g, qi, ki: (b * nq + qi, g)),
            pl.BlockSpec((tk, HEAD_DIM), lambda b, g, qi, ki: (b * nk + ki, g)),
            pl.BlockSpec((tk, 2 * HEAD_DIM), lambda b, g, qi, ki: (b * nk + ki, g)),
        ],
        out_specs=[
            pl.BlockSpec((tq, group_w), lambda b, g, qi, ki: (b * nq + qi, g)),
            pl.BlockSpec((SUBLANES, HEAD_DIM), lambda b, g, qi, ki: (b * nq + qi, g)),
        ],
        out_shape=[
            jax.ShapeDtypeStruct((t, Q_WIDTH), BF16),
            jax.ShapeDtypeStruct((batch * nq * SUBLANES, N_KV_HEADS * HEAD_DIM), F32),
        ],
        scratch_shapes=[
            pltpu.VMEM((GQA_GROUP, tq, HEAD_DIM), F32),
            pltpu.VMEM((GQA_GROUP, tq, 2 * HEAD_DIM), F32),
        ],
        compiler_params=_params(("arbitrary", "arbitrary", "arbitrary", "arbitrary")),
        name="flash_fast",
    )(kmax, q, k, v_ext)


def _flash(q, k, v_ext, batch, seq, tq, tk):
    t = q.shape[0]
    nq = seq // tq
    nk = seq // tk
    group_w = GQA_GROUP * HEAD_DIM
    return pl.pallas_call(
        _flash_kernel,
        grid=(batch, N_KV_HEADS, nq, nk),
        in_specs=[
            pl.BlockSpec((tq, group_w), lambda b, g, qi, ki: (b * nq + qi, g)),
            pl.BlockSpec((tk, HEAD_DIM), lambda b, g, qi, ki: (b * nk + ki, g)),
            pl.BlockSpec((tk, HEAD_DIM), lambda b, g, qi, ki: (b * nk + ki, 2 * g)),
        ],
        out_specs=pl.BlockSpec((tq, group_w), lambda b, g, qi, ki: (b * nq + qi, g)),
        out_shape=jax.ShapeDtypeStruct((t, Q_WIDTH), BF16),
        scratch_shapes=[
            pltpu.VMEM((GQA_GROUP, tq, HEAD_DIM), F32),
            pltpu.VMEM((GQA_GROUP, tq, HEAD_DIM), F32),
            pltpu.VMEM((GQA_GROUP, tq, HEAD_DIM), F32),
        ],
        compiler_params=_params(("arbitrary", "arbitrary", "arbitrary", "arbitrary")),
        name="flash",
    )(q, k, v_ext)


def _attention(q, k, v_ext, kn2, batch, seq, tm, tq, tk, tq_fast, tk_fast):
    kmax = jnp.sqrt(jnp.max(
        kn2.reshape(batch, seq // tm, SUBLANES, N_KV_HEADS, HEAD_DIM)[:, :, 0, :, 0], axis=1))
    y_fast, lmin = _flash_fast(kmax * KEY_NORM_MARGIN, q, k, v_ext, batch, seq, tq_fast, tk_fast)
    trusted = jnp.all(lmin >= ROW_SUM_FLOOR)
    return lax.cond(trusted, lambda: y_fast, lambda: _flash(q, k, v_ext, batch, seq, tq, tk))


def _gatemix_kernel(h_ref, ysgu_ref, yattn_ref, ymem_ref, wg0_ref, wg1_ref, wg2_ref, gb_ref,
                    wbs_ref, wba_ref, wbm_ref, y_ref):
    h = h_ref[...]
    y = jax.nn.sigmoid(_dot(h, wg0_ref[...]) + gb_ref[0:1, :]) * _dot(ysgu_ref[...], wbs_ref[...])
    y += jax.nn.sigmoid(_dot(h, wg1_ref[...]) + gb_ref[1:2, :]) * _dot(yattn_ref[...], wba_ref[...])
    y += jax.nn.sigmoid(_dot(h, wg2_ref[...]) + gb_ref[2:3, :]) * _dot(ymem_ref[...], wbm_ref[...])
    y_ref[...] = y.astype(BF16)


def _gatemix(h, y_sgu, y_attn, y_mem, w_in, gate_b, w_br_sgu, w_br_attn, w_br_mem, tm, tn, layer):
    t, d = h.shape
    nj = d // tn
    gate0 = MIX_WIDTH // tn
    row = lambda i, j: (i, 0)
    lcol = lambda i, j: (layer, 0, j)
    return pl.pallas_call(
        _gatemix_kernel,
        grid=(t // tm, nj),
        in_specs=[
            pl.BlockSpec((tm, d), row),
            pl.BlockSpec((tm, SGU_WIDTH), row),
            pl.BlockSpec((tm, Q_WIDTH), row),
            pl.BlockSpec((tm, MEM_WIDTH), row),
            pl.BlockSpec((None, d, tn), lambda i, j: (layer, 0, gate0 + j)),
            pl.BlockSpec((None, d, tn), lambda i, j: (layer, 0, gate0 + nj + j)),
            pl.BlockSpec((None, d, tn), lambda i, j: (layer, 0, gate0 + 2 * nj + j)),
            pl.BlockSpec((N_BRANCH, tn), lambda i, j: (0, j)),
            pl.BlockSpec((None, SGU_WIDTH, tn), lcol),
            pl.BlockSpec((None, Q_WIDTH, tn), lcol),
            pl.BlockSpec((None, MEM_WIDTH, tn), lcol),
        ],
        out_specs=pl.BlockSpec((tm, tn), lambda i, j: (i, j)),
        out_shape=jax.ShapeDtypeStruct((t, d), BF16),
        compiler_params=_params(("arbitrary", "arbitrary")),
        name="gatemix",
    )(h, y_sgu, y_attn, y_mem, w_in, w_in, w_in, gate_b, w_br_sgu, w_br_attn, w_br_mem)


def _outproj_kernel(x_ref, y_ref, w_ref, o_ref):
    o_ref[...] = x_ref[...] + _dot(y_ref[...], w_ref[...])


def _outproj(x2d, y, w_out, tm, layer):
    t, d = x2d.shape
    row = lambda i: (i, 0)
    return pl.pallas_call(
        _outproj_kernel,
        grid=(t // tm,),
        in_specs=[
            pl.BlockSpec((tm, d), row),
            pl.BlockSpec((tm, d), row),
            pl.BlockSpec((None, d, d), lambda i: (layer, 0, 0), pipeline_mode=pl.Buffered(1)),
        ],
        out_specs=pl.BlockSpec((tm, d), row),
        out_shape=jax.ShapeDtypeStruct((t, d), F32),
        compiler_params=_params(("arbitrary",)),
        name="outproj",
    )(x2d, y, w_out)


def _ffn_kernel(x_ref, nffn_ref, wg_ref, wu_ref, wd_ref, o_ref, h_sc):
    j = pl.program_id(1)

    @pl.when(j == 0)
    def _():
        x = x_ref[...]
        h_sc[...] = _rms(x, nffn_ref[...]).astype(BF16)
        o_ref[...] = x

    h = h_sc[...]
    a = _dot(h, wg_ref[...])
    b = _dot(h, wu_ref[...])
    o_ref[...] += _dot((a * jax.nn.sigmoid(a) * b).astype(BF16), wd_ref[...])


def _ffn(x2d, norm_ffn, w_gate_up, w_down, tm, tf, layer):
    t, d = x2d.shape
    d_ff = w_down.shape[1]
    nj = d_ff // tf
    row = lambda i, j: (i, 0)
    return pl.pallas_call(
        _ffn_kernel,
        grid=(t // tm, nj),
        in_specs=[
            pl.BlockSpec((tm, d), row),
            pl.BlockSpec((1, d), lambda i, j: (0, 0)),
            pl.BlockSpec((None, d, tf), lambda i, j: (layer, 0, j)),
            pl.BlockSpec((None, d, tf), lambda i, j: (layer, 0, nj + j)),
            pl.BlockSpec((None, tf, d), lambda i, j: (layer, j, 0)),
        ],
        out_specs=pl.BlockSpec((tm, d), row),
        out_shape=jax.ShapeDtypeStruct((t, d), F32),
        scratch_shapes=[pltpu.VMEM((tm, d), BF16)],
        compiler_params=_params(("arbitrary", "arbitrary")),
        name="ffn",
    )(x2d, norm_ffn, w_gate_up, w_gate_up, w_down)


def _rope_tables(seq):
    quarter = HEAD_DIM // 4
    inv = ROPE_THETA ** (-jnp.arange(0, 2 * quarter, 2, dtype=F32) / (2 * quarter))
    pos = jnp.arange(seq, dtype=jnp.int32)
    ar = (pos // GRID_W).astype(F32)[:, None] * inv[None, :]
    ac = (pos % GRID_W).astype(F32)[:, None] * inv[None, :]
    cos_t = jnp.concatenate([jnp.cos(ar), jnp.cos(ar), jnp.cos(ac), jnp.cos(ac)], axis=-1)
    sin_t = jnp.concatenate([-jnp.sin(ar), jnp.sin(ar), -jnp.sin(ac), jnp.sin(ac)], axis=-1)
    return cos_t, sin_t


def _tile(n, want):
    if n <= want:
        return n
    for cand in range(want, 0, -128):
        if n % cand == 0:
            return cand
    return n


def kernel(x, mem, norm_mix, w_in, gate_b, sgu_norm, sgu_w, sgu_b, q_norm, k_norm, mem_norm,
           w_mem_kv, mq_norm, mk_norm, w_br_sgu, w_br_attn, w_br_mem, w_out, norm_ffn,
           w_gate_up, w_down):
    batch, seq, d = x.shape
    n_mem = mem.shape[1]
    depth = w_in.shape[0]
    assert seq % SGU_CHUNK == 0 and seq % GRID_W == 0

    tm = _tile(seq, 512)
    tq = _tile(seq, 512)
    tk = _tile(seq, 1024)
    tq_fast = _tile(seq, 2048)
    tk_fast = _tile(seq, 4096)
    tn = _tile(d, 512)
    tm_big = _tile(seq, 1024)
    tf = _tile(w_down.shape[1], 512)

    cos_t, sin_t = _rope_tables(seq)
    x2d = x.reshape(batch * seq, d)
    mem2d = mem.reshape(batch * n_mem, d)
    row = lambda a: a.reshape(1, -1)

    w_in_b, w_mem_kv_b, sgu_w_b = w_in.astype(BF16), w_mem_kv.astype(BF16), sgu_w.astype(BF16)
    w_br_sgu_b, w_br_attn_b, w_br_mem_b = (w.astype(BF16) for w in (w_br_sgu, w_br_attn, w_br_mem))
    w_out_b, w_gate_up_b, w_down_b = w_out.astype(BF16), w_gate_up.astype(BF16), w_down.astype(BF16)

    for l in range(depth):
        sgu_b_full = jnp.repeat(sgu_b[l].T, HEAD_DIM, axis=1)
        mk, mv = _memkv(mem2d, row(mem_norm[l]), w_mem_kv_b, row(mk_norm[l]), n_mem, l)
        y_sgu, q, k, v_ext, y_mem, kn2, h = _inproj(
            x2d, row(norm_mix[l]), w_in_b, cos_t, sin_t, row(sgu_norm[l]), sgu_w_b, sgu_b_full,
            row(q_norm[l]), row(k_norm[l]), row(mq_norm[l]), mk, mv, seq, n_mem, tm, l)
        y_attn = _attention(q, k, v_ext, kn2, batch, seq, tm, tq, tk, tq_fast, tk_fast)
        y = _gatemix(h, y_sgu, y_attn, y_mem, w_in_b, gate_b[l], w_br_sgu_b, w_br_attn_b, w_br_mem_b,
                     tm_big, tn, l)
        x2d = _outproj(x2d, y, w_out_b, tm_big, l)
        x2d = _ffn(x2d, row(norm_ffn[l]), w_gate_up_b, w_down_b, tm_big, tf, l)
    return x2d.reshape(batch, seq, d)
```

```python
import math

import jax
import jax.numpy as jnp
from jax import lax
from jax.experimental import pallas as pl
from jax.experimental.pallas import tpu as pltpu

HEAD_DIM = 128
N_Q_HEADS = 8
N_KV_HEADS = 2
GQA_GROUP = N_Q_HEADS // N_KV_HEADS
N_MEM_HEADS = 4
SGU_GROUPS = 4
SGU_CHUNK = 128
GRID_W = 64
ROPE_THETA = 10000.0
N_BRANCH = 3
EPS = 1e-6

SGU_WIDTH = SGU_GROUPS * HEAD_DIM
Q_WIDTH = N_Q_HEADS * HEAD_DIM
KV_WIDTH = N_KV_HEADS * HEAD_DIM
MEM_WIDTH = N_MEM_HEADS * HEAD_DIM
MIX_WIDTH = 2 * SGU_WIDTH + Q_WIDTH + 2 * KV_WIDTH + MEM_WIDTH
OFF_Q = 2 * SGU_WIDTH
OFF_K = OFF_Q + Q_WIDTH
OFF_V = OFF_K + KV_WIDTH
OFF_MQ = OFF_V + KV_WIDTH

LOG2E = math.log2(math.e)
ATTN_SCALE = HEAD_DIM ** -0.5

V7X_VMEM_LIMIT_BYTES = 56 * 1024 * 1024
SUBLANES = 8

INPROJ_ROWS = 512
DENSE_ROWS = 1024
WEIGHT_COLS = 512
FLASH_ROWS = 2048
ONLINE_Q_ROWS = 512
ONLINE_K_ROWS = 1024

STAB_SLACK = 60.0
ROW_SUM_FLOOR = 2.0 ** -80
KEY_NORM_MARGIN = 1.01

BF16 = jnp.bfloat16
F32 = jnp.float32


def _dot(a, b):
    return jnp.dot(a, b, preferred_element_type=F32)


def _dot_nt(a, b):
    return lax.dot_general(a, b, (((1,), (1,)), ((), ())), preferred_element_type=F32)


def _rms(z, gain):
    return z * lax.rsqrt(jnp.mean(z * z, axis=-1, keepdims=True) + EPS) * gain


def _gelu(z):
    return 0.5 * z * (1.0 + lax.erf(z * math.sqrt(0.5)))


def _rope(z, cos, sin_signed, first_of_pair):
    partner = jnp.where(first_of_pair, pltpu.roll(z, 96, 1), pltpu.roll(z, 32, 1))
    return z * cos + partner * sin_signed


def _params(sem):
    return pltpu.CompilerParams(dimension_semantics=sem, vmem_limit_bytes=V7X_VMEM_LIMIT_BYTES)


def _memkv_kernel(mem_ref, mnorm_ref, w_ref, mkn_ref, mk_ref, mv_ref):
    hm = _rms(mem_ref[...], mnorm_ref[...]).astype(BF16)
    kv = _dot(hm, w_ref[...])
    for hh in range(N_MEM_HEADS):
        sl = slice(hh * HEAD_DIM, (hh + 1) * HEAD_DIM)
        mk_ref[:, sl] = _rms(kv[:, sl], mkn_ref[...]).astype(BF16)
    mv_ref[...] = kv[:, MEM_WIDTH:].astype(BF16)


def _memkv(mem2d, mem_norm, w_mem_kv, mk_norm, n_mem, layer):
    rows, d = mem2d.shape
    return pl.pallas_call(
        _memkv_kernel,
        grid=(rows // n_mem,),
        in_specs=[
            pl.BlockSpec((n_mem, d), lambda i: (i, 0)),
            pl.BlockSpec((1, d), lambda i: (0, 0)),
            pl.BlockSpec((None, d, 2 * MEM_WIDTH), lambda i: (layer, 0, 0)),
            pl.BlockSpec((1, HEAD_DIM), lambda i: (0, 0)),
        ],
        out_specs=[
            pl.BlockSpec((n_mem, MEM_WIDTH), lambda i: (i, 0)),
            pl.BlockSpec((n_mem, MEM_WIDTH), lambda i: (i, 0)),
        ],
        out_shape=[jax.ShapeDtypeStruct((rows, MEM_WIDTH), BF16)] * 2,
        compiler_params=_params(("arbitrary",)),
        name="memkv",
    )(mem2d, mem_norm, w_mem_kv, mk_norm)


def _inproj_kernel(x_ref, nmix_ref, w_ref, cos_ref, sin_ref, sgun_ref, sguw_ref, sgub_ref,
                   qn_ref, kn_ref, mqn_ref, mk_ref, mv_ref,
                   ysgu_ref, q_ref, k_ref, v_ref, ymem_ref, kn2_ref, h_ref, za_sc, zb_sc):
    i = pl.program_id(0)
    rest = (cos_ref, sin_ref, sgun_ref, sguw_ref, sgub_ref, qn_ref, kn_ref, mqn_ref, mk_ref, mv_ref,
            ysgu_ref, q_ref, k_ref, v_ref, ymem_ref, kn2_ref)

    @pl.when(i == 0)
    def _():
        zb_sc[...] = jnp.zeros(zb_sc.shape, F32)

    def step(z_write, z_read):
        h = _rms(x_ref[...], nmix_ref[...]).astype(BF16)
        h_ref[...] = h
        z_write[...] = _dot(h, w_ref[...])
        _inproj_finish(z_read, *rest)

    @pl.when(i % 2 == 0)
    def _():
        step(za_sc, zb_sc)

    @pl.when(i % 2 == 1)
    def _():
        step(zb_sc, za_sc)


def _inproj_finish(z, cos_ref, sin_ref, sgun_ref, sguw_ref, sgub_ref, qn_ref, kn_ref, mqn_ref,
                   mk_ref, mv_ref, ysgu_ref, q_ref, k_ref, v_ref, ymem_ref, kn2_ref):
    tm = z.shape[0]

    zs = _gelu(z[:, 0:OFF_Q])
    for g in range(SGU_GROUPS):
        sl = slice(g * HEAD_DIM, (g + 1) * HEAD_DIM)
        u_g = zs[:, sl]
        vs_g = _rms(zs[:, SGU_WIDTH + g * HEAD_DIM:SGU_WIDTH + (g + 1) * HEAD_DIM],
                    sgun_ref[...]).astype(BF16)
        w_g = sguw_ref[g]
        for c in range(tm // SGU_CHUNK):
            rows = slice(c * SGU_CHUNK, (c + 1) * SGU_CHUNK)
            sp = _dot(w_g, vs_g[rows, :]) + sgub_ref[:, sl]
            ysgu_ref[rows, sl] = (u_g[rows, :] * sp).astype(BF16)

    cos = cos_ref[...]
    sin_signed = sin_ref[...]
    lane = lax.broadcasted_iota(jnp.int32, (tm, HEAD_DIM), 1)
    first_of_pair = (lane & 32) == 0
    zq = z[:, OFF_Q:OFF_K]
    for hh in range(N_Q_HEADS):
        sl = slice(hh * HEAD_DIM, (hh + 1) * HEAD_DIM)
        qh = _rope(_rms(zq[:, sl], qn_ref[...]), cos, sin_signed, first_of_pair)
        q_ref[:, sl] = (qh * (ATTN_SCALE * LOG2E)).astype(BF16)
    zkv = z[:, OFF_K:OFF_MQ]
    for hh in range(N_KV_HEADS):
        sl = slice(hh * HEAD_DIM, (hh + 1) * HEAD_DIM)
        kh = _rope(_rms(zkv[:, sl], kn_ref[...]), cos, sin_signed, first_of_pair)
        k_ref[:, sl] = kh.astype(BF16)
        kn2 = jnp.max(jnp.sum(kh * kh, axis=-1, keepdims=True), axis=0, keepdims=True)
        kn2_ref[:, sl] = jnp.broadcast_to(kn2, (kn2_ref.shape[0], HEAD_DIM))
        v_ref[:, 2 * hh * HEAD_DIM:(2 * hh + 1) * HEAD_DIM] = zkv[:, KV_WIDTH + hh * HEAD_DIM:
                                                                   KV_WIDTH + (hh + 1) * HEAD_DIM].astype(BF16)
        v_ref[:, (2 * hh + 1) * HEAD_DIM:(2 * hh + 2) * HEAD_DIM] = jnp.ones((tm, HEAD_DIM), BF16)

    zmq = z[:, OFF_MQ:MIX_WIDTH]
    for hh in range(N_MEM_HEADS):
        sl = slice(hh * HEAD_DIM, (hh + 1) * HEAD_DIM)
        mq = (_rms(zmq[:, sl], mqn_ref[...]) * (ATTN_SCALE * LOG2E)).astype(BF16)
        s = _dot_nt(mq, mk_ref[:, sl])
        p = jnp.exp2(s - jnp.max(s, axis=-1, keepdims=True))
        o = _dot(p.astype(BF16), mv_ref[:, sl])
        ymem_ref[:, sl] = (o / jnp.sum(p, axis=-1, keepdims=True)).astype(BF16)


def _inproj(x2d, norm_mix, w_in, cos_t, sin_t, sgu_norm, sgu_w, sgu_b_full, q_norm, k_norm,
            mq_norm, mk, mv, seq, n_mem, tm, layer):
    t, d = x2d.shape
    tiles_per_seq = seq // tm
    n_tiles = t // tm
    const = lambda i: (0, 0)
    prev = lambda i: jnp.maximum(i - 1, 0)
    row = lambda i: (prev(i), 0)
    cur = lambda i: (jnp.minimum(i, n_tiles - 1), 0)
    return pl.pallas_call(
        _inproj_kernel,
        grid=(n_tiles + 1,),
        in_specs=[
            pl.BlockSpec((tm, d), cur),
            pl.BlockSpec((1, d), const),
            pl.BlockSpec((None, d, MIX_WIDTH), lambda i: (layer, 0, 0), pipeline_mode=pl.Buffered(1)),
            pl.BlockSpec((tm, HEAD_DIM), lambda i: (prev(i) % tiles_per_seq, 0)),
            pl.BlockSpec((tm, HEAD_DIM), lambda i: (prev(i) % tiles_per_seq, 0)),
            pl.BlockSpec((1, HEAD_DIM), const),
            pl.BlockSpec((None, SGU_GROUPS, SGU_CHUNK, SGU_CHUNK), lambda i: (layer, 0, 0, 0)),
            pl.BlockSpec((SGU_CHUNK, SGU_WIDTH), const),
            pl.BlockSpec((1, HEAD_DIM), const),
            pl.BlockSpec((1, HEAD_DIM), const),
            pl.BlockSpec((1, HEAD_DIM), const),
            pl.BlockSpec((n_mem, MEM_WIDTH), lambda i: (prev(i) // tiles_per_seq, 0)),
            pl.BlockSpec((n_mem, MEM_WIDTH), lambda i: (prev(i) // tiles_per_seq, 0)),
        ],
        out_specs=[
            pl.BlockSpec((tm, SGU_WIDTH), row),
            pl.BlockSpec((tm, Q_WIDTH), row),
            pl.BlockSpec((tm, KV_WIDTH), row),
            pl.BlockSpec((tm, 2 * KV_WIDTH), row),
            pl.BlockSpec((tm, MEM_WIDTH), row),
            pl.BlockSpec((SUBLANES, KV_WIDTH), row),
            pl.BlockSpec((tm, d), cur),
        ],
        out_shape=[
            jax.ShapeDtypeStruct((t, SGU_WIDTH), BF16),
            jax.ShapeDtypeStruct((t, Q_WIDTH), BF16),
            jax.ShapeDtypeStruct((t, KV_WIDTH), BF16),
            jax.ShapeDtypeStruct((t, 2 * KV_WIDTH), BF16),
            jax.ShapeDtypeStruct((t, MEM_WIDTH), BF16),
            jax.ShapeDtypeStruct((t // tm * SUBLANES, KV_WIDTH), F32),
            jax.ShapeDtypeStruct((t, d), BF16),
        ],
        scratch_shapes=[pltpu.VMEM((tm, MIX_WIDTH), F32), pltpu.VMEM((tm, MIX_WIDTH), F32)],
        compiler_params=_params(("arbitrary",)),
        name="inproj",
    )(x2d, norm_mix, w_in, cos_t, sin_t, sgu_norm, sgu_w, sgu_b_full, q_norm, k_norm, mq_norm, mk, mv)


def _flash_kernel(q_ref, k_ref, v_ref, o_ref, m_sc, l_sc, acc_sc):
    ki = pl.program_id(3)

    @pl.when(ki == 0)
    def _():
        m_sc[...] = jnp.full(m_sc.shape, -jnp.inf, F32)
        l_sc[...] = jnp.zeros(l_sc.shape, F32)
        acc_sc[...] = jnp.zeros(acc_sc.shape, F32)

    k = k_ref[...]
    v = v_ref[...]
    for hh in range(GQA_GROUP):
        sl = slice(hh * HEAD_DIM, (hh + 1) * HEAD_DIM)
        s = _dot_nt(q_ref[:, sl], k)
        m_prev = m_sc[hh]
        m_new = jnp.maximum(m_prev, jnp.max(s, axis=-1, keepdims=True))
        alpha = jnp.exp2(m_prev - m_new)
        p = jnp.exp2(s - m_new[:, 0:1])
        l_sc[hh] = alpha * l_sc[hh] + jnp.sum(p, axis=-1, keepdims=True)
        acc_sc[hh] = alpha * acc_sc[hh] + _dot(p.astype(BF16), v)
        m_sc[hh] = m_new

    @pl.when(ki == pl.num_programs(3) - 1)
    def _():
        for hh in range(GQA_GROUP):
            sl = slice(hh * HEAD_DIM, (hh + 1) * HEAD_DIM)
            o_ref[:, sl] = (acc_sc[hh] / l_sc[hh]).astype(BF16)


def _flash_fast_kernel(kmax_ref, q_ref, k_ref, v_ref, o_ref, lmin_ref, c_sc, acc_sc):
    ki = pl.program_id(3)
    tq = q_ref.shape[0]
    tk = k_ref.shape[0]

    @pl.when(ki == 0)
    def _():
        kmax = kmax_ref[pl.program_id(0), pl.program_id(1)]
        for hh in range(GQA_GROUP):
            qf = q_ref[:, hh * HEAD_DIM:(hh + 1) * HEAD_DIM].astype(F32)
            qn = jnp.sqrt(jnp.sum(qf * qf, axis=-1, keepdims=True))
            c_sc[hh] = jnp.broadcast_to(qn * kmax - STAB_SLACK, (tq, HEAD_DIM))
        acc_sc[...] = jnp.zeros(acc_sc.shape, F32)

    k = k_ref[...]
    v = v_ref[...]
    for hh in range(GQA_GROUP):
        s = _dot_nt(q_ref[:, hh * HEAD_DIM:(hh + 1) * HEAD_DIM], k)
        p = jnp.exp2(s - jnp.tile(c_sc[hh], (1, tk // HEAD_DIM)))
        acc_sc[hh] += _dot(p.astype(BF16), v)

    @pl.when(ki == pl.num_programs(3) - 1)
    def _():
        lmin = None
        for hh in range(GQA_GROUP):
            acc = acc_sc[hh]
            l = acc[:, HEAD_DIM:]
            o_ref[:, hh * HEAD_DIM:(hh + 1) * HEAD_DIM] = (acc[:, :HEAD_DIM] / l).astype(BF16)
            lm = jnp.min(l, axis=0, keepdims=True)
            lmin = lm if lmin is None else jnp.minimum(lmin, lm)
        lmin_ref[...] = jnp.broadcast_to(lmin, lmin_ref.shape)


def _flash_fast(kmax, q, k, v_ext, batch, seq, tq, tk):
    t = q.shape[0]
    nq = seq // tq
    nk = seq // tk
    group_w = GQA_GROUP * HEAD_DIM
    return pl.pallas_call(
        _flash_fast_kernel,
        grid=(batch, N_KV_HEADS, nq, nk),
        in_specs=[
            pl.BlockSpec(memory_space=pltpu.SMEM),
            pl.BlockSpec((tq, group_w), lambda b, g, qi, ki: (b * nq + qi, g)),
            pl.BlockSpec((tk, HEAD_DIM), lambda b, g, qi, ki: (b * nk + ki, g)),
            pl.BlockSpec((tk, 2 * HEAD_DIM), lambda b, g, qi, ki: (b * nk + ki, g)),
        ],
        out_specs=[
            pl.BlockSpec((tq, group_w), lambda b, g, qi, ki: (b * nq + qi, g)),
            pl.BlockSpec((SUBLANES, HEAD_DIM), lambda b, g, qi, ki: (b * nq + qi, g)),
        ],
        out_shape=[
            jax.ShapeDtypeStruct((t, Q_WIDTH), BF16),
            jax.ShapeDtypeStruct((batch * nq * SUBLANES, N_KV_HEADS * HEAD_DIM), F32),
        ],
        scratch_shapes=[
            pltpu.VMEM((GQA_GROUP, tq, HEAD_DIM), F32),
            pltpu.VMEM((GQA_GROUP, tq, 2 * HEAD_DIM), F32),
        ],
        compiler_params=_params(("arbitrary", "arbitrary", "arbitrary", "arbitrary")),
        name="flash_fast",
    )(kmax, q, k, v_ext)


def _flash(q, k, v_ext, batch, seq, tq, tk):
    t = q.shape[0]
    nq = seq // tq
    nk = seq // tk
    group_w = GQA_GROUP * HEAD_DIM
    return pl.pallas_call(
        _flash_kernel,
        grid=(batch, N_KV_HEADS, nq, nk),
        in_specs=[
            pl.BlockSpec((tq, group_w), lambda b, g, qi, ki: (b * nq + qi, g)),
            pl.BlockSpec((tk, HEAD_DIM), lambda b, g, qi, ki: (b * nk + ki, g)),
            pl.BlockSpec((tk, HEAD_DIM), lambda b, g, qi, ki: (b * nk + ki, 2 * g)),
        ],
        out_specs=pl.BlockSpec((tq, group_w), lambda b, g, qi, ki: (b * nq + qi, g)),
        out_shape=jax.ShapeDtypeStruct((t, Q_WIDTH), BF16),
        scratch_shapes=[
            pltpu.VMEM((GQA_GROUP, tq, HEAD_DIM), F32),
            pltpu.VMEM((GQA_GROUP, tq, HEAD_DIM), F32),
            pltpu.VMEM((GQA_GROUP, tq, HEAD_DIM), F32),
        ],
        compiler_params=_params(("arbitrary", "arbitrary", "arbitrary", "arbitrary")),
        name="flash",
    )(q, k, v_ext)


def _attention(q, k, v_ext, kn2, batch, seq, tm, tq, tk, tq_fast, tk_fast):
    kmax = jnp.sqrt(jnp.max(
        kn2.reshape(batch, seq // tm * SUBLANES, N_KV_HEADS, HEAD_DIM), axis=(1, 3)))
    y_fast, lmin = _flash_fast(kmax * KEY_NORM_MARGIN, q, k, v_ext, batch, seq, tq_fast, tk_fast)
    trusted = jnp.all(lmin >= ROW_SUM_FLOOR)
    return lax.cond(trusted, lambda: y_fast, lambda: _flash(q, k, v_ext, batch, seq, tq, tk))


def _gatemix_kernel(h_ref, ysgu_ref, yattn_ref, ymem_ref, wg0_ref, wg1_ref, wg2_ref, gb_ref,
                    wbs_ref, wba_ref, wbm_ref, y_ref):
    h = h_ref[...]
    y = jax.nn.sigmoid(_dot(h, wg0_ref[...]) + gb_ref[0:1, :]) * _dot(ysgu_ref[...], wbs_ref[...])
    y += jax.nn.sigmoid(_dot(h, wg1_ref[...]) + gb_ref[1:2, :]) * _dot(yattn_ref[...], wba_ref[...])
    y += jax.nn.sigmoid(_dot(h, wg2_ref[...]) + gb_ref[2:3, :]) * _dot(ymem_ref[...], wbm_ref[...])
    y_ref[...] = y.astype(BF16)


def _gatemix(h, y_sgu, y_attn, y_mem, w_in, gate_b, w_br_sgu, w_br_attn, w_br_mem, tm, tn, layer):
    t, d = h.shape
    nj = d // tn
    gate0 = MIX_WIDTH // tn
    row = lambda i, j: (i, 0)
    lcol = lambda i, j: (layer, 0, j)
    return pl.pallas_call(
        _gatemix_kernel,
        grid=(t // tm, nj),
        in_specs=[
            pl.BlockSpec((tm, d), row),
            pl.BlockSpec((tm, SGU_WIDTH), row),
            pl.BlockSpec((tm, Q_WIDTH), row),
            pl.BlockSpec((tm, MEM_WIDTH), row),
            pl.BlockSpec((None, d, tn), lambda i, j: (layer, 0, gate0 + j)),
            pl.BlockSpec((None, d, tn), lambda i, j: (layer, 0, gate0 + nj + j)),
            pl.BlockSpec((None, d, tn), lambda i, j: (layer, 0, gate0 + 2 * nj + j)),
            pl.BlockSpec((N_BRANCH, tn), lambda i, j: (0, j)),
            pl.BlockSpec((None, SGU_WIDTH, tn), lcol),
            pl.BlockSpec((None, Q_WIDTH, tn), lcol),
            pl.BlockSpec((None, MEM_WIDTH, tn), lcol),
        ],
        out_specs=pl.BlockSpec((tm, tn), lambda i, j: (i, j)),
        out_shape=jax.ShapeDtypeStruct((t, d), BF16),
        compiler_params=_params(("arbitrary", "arbitrary")),
        name="gatemix",
    )(h, y_sgu, y_attn, y_mem, w_in, w_in, w_in, gate_b, w_br_sgu, w_br_attn, w_br_mem)


def _outproj_kernel(x_ref, y_ref, w_ref, o_ref):
    o_ref[...] = x_ref[...] + _dot(y_ref[...], w_ref[...])


def _outproj(x2d, y, w_out, tm, layer):
    t, d = x2d.shape
    row = lambda i: (i, 0)
    return pl.pallas_call(
        _outproj_kernel,
        grid=(t // tm,),
        in_specs=[
            pl.BlockSpec((tm, d), row),
            pl.BlockSpec((tm, d), row),
            pl.BlockSpec((None, d, d), lambda i: (layer, 0, 0), pipeline_mode=pl.Buffered(1)),
        ],
        out_specs=pl.BlockSpec((tm, d), row),
        out_shape=jax.ShapeDtypeStruct((t, d), F32),
        compiler_params=_params(("arbitrary",)),
        name="outproj",
    )(x2d, y, w_out)


def _ffn_kernel(x_ref, nffn_ref, wg_ref, wu_ref, wd_ref, o_ref, h_sc):
    j = pl.program_id(1)

    @pl.when(j == 0)
    def _():
        x = x_ref[...]
        h_sc[...] = _rms(x, nffn_ref[...]).astype(BF16)
        o_ref[...] = x

    h = h_sc[...]
    a = _dot(h, wg_ref[...])
    b = _dot(h, wu_ref[...])
    o_ref[...] += _dot((a * jax.nn.sigmoid(a) * b).astype(BF16), wd_ref[...])


def _ffn(x2d, norm_ffn, w_gate_up, w_down, tm, tf, layer):
    t, d = x2d.shape
    d_ff = w_down.shape[1]
    nj = d_ff // tf
    row = lambda i, j: (i, 0)
    return pl.pallas_call(
        _ffn_kernel,
        grid=(t // tm, nj),
        in_specs=[
            pl.BlockSpec((tm, d), row),
            pl.BlockSpec((1, d), lambda i, j: (0, 0)),
            pl.BlockSpec((None, d, tf), lambda i, j: (layer, 0, j)),
            pl.BlockSpec((None, d, tf), lambda i, j: (layer, 0, nj + j)),
            pl.BlockSpec((None, tf, d), lambda i, j: (layer, j, 0)),
        ],
        out_specs=pl.BlockSpec((tm, d), row),
        out_shape=jax.ShapeDtypeStruct((t, d), F32),
        scratch_shapes=[pltpu.VMEM((tm, d), BF16)],
        compiler_params=_params(("arbitrary", "arbitrary")),
        name="ffn",
    )(x2d, norm_ffn, w_gate_up, w_gate_up, w_down)


def _rope_tables(seq):
    quarter = HEAD_DIM // 4
    inv = ROPE_THETA ** (-jnp.arange(0, 2 * quarter, 2, dtype=F32) / (2 * quarter))
    rows = seq // GRID_W
    ar = jnp.arange(rows, dtype=F32)[:, None] * inv[None, :]
    ac = jnp.arange(GRID_W, dtype=F32)[:, None] * inv[None, :]
    by_row = lambda a: jnp.repeat(a, GRID_W, axis=0)
    by_col = lambda a: jnp.tile(a, (rows, 1))
    cr, sr, cc, sc = by_row(jnp.cos(ar)), by_row(jnp.sin(ar)), by_col(jnp.cos(ac)), by_col(jnp.sin(ac))
    cos_t = jnp.concatenate([cr, cr, cc, cc], axis=-1)
    sin_t = jnp.concatenate([-sr, sr, -sc, sc], axis=-1)
    return cos_t, sin_t


def _tile(n, want):
    if n <= want:
        return n
    for cand in range(want, 0, -128):
        if n % cand == 0:
            return cand
    return n


def kernel(x, mem, norm_mix, w_in, gate_b, sgu_norm, sgu_w, sgu_b, q_norm, k_norm, mem_norm,
           w_mem_kv, mq_norm, mk_norm, w_br_sgu, w_br_attn, w_br_mem, w_out, norm_ffn,
           w_gate_up, w_down):
    batch, seq, d = x.shape
    n_mem = mem.shape[1]
    depth = w_in.shape[0]
    assert seq % SGU_CHUNK == 0 and seq % GRID_W == 0

    tm = _tile(seq, INPROJ_ROWS)
    tq = _tile(seq, ONLINE_Q_ROWS)
    tk = _tile(seq, ONLINE_K_ROWS)
    tq_fast = _tile(seq, FLASH_ROWS)
    tk_fast = _tile(seq, FLASH_ROWS)
    tn = _tile(d, WEIGHT_COLS)
    tm_big = _tile(seq, DENSE_ROWS)
    tf = _tile(w_down.shape[1], WEIGHT_COLS)

    cos_t, sin_t = _rope_tables(seq)
    x2d = x.reshape(batch * seq, d)
    mem2d = mem.reshape(batch * n_mem, d)
    row = lambda a: a.reshape(1, -1)

    w_in_b, w_mem_kv_b, sgu_w_b = w_in.astype(BF16), w_mem_kv.astype(BF16), sgu_w.astype(BF16)
    w_br_sgu_b, w_br_attn_b, w_br_mem_b = (w.astype(BF16) for w in (w_br_sgu, w_br_attn, w_br_mem))
    w_out_b, w_gate_up_b, w_down_b = w_out.astype(BF16), w_gate_up.astype(BF16), w_down.astype(BF16)

    for l in range(depth):
        sgu_b_full = jnp.repeat(sgu_b[l].T, HEAD_DIM, axis=1)
        mk, mv = _memkv(mem2d, row(mem_norm[l]), w_mem_kv_b, row(mk_norm[l]), n_mem, l)
        y_sgu, q, k, v_ext, y_mem, kn2, h = _inproj(
            x2d, row(norm_mix[l]), w_in_b, cos_t, sin_t, row(sgu_norm[l]), sgu_w_b, sgu_b_full,
            row(q_norm[l]), row(k_norm[l]), row(mq_norm[l]), mk, mv, seq, n_mem, tm, l)
        y_attn = _attention(q, k, v_ext, kn2, batch, seq, tm, tq, tk, tq_fast, tk_fast)
        y = _gatemix(h, y_sgu, y_attn, y_mem, w_in_b, gate_b[l], w_br_sgu_b, w_br_attn_b, w_br_mem_b,
                     tm_big, tn, l)
        x2d = _outproj(x2d, y, w_out_b, tm_big, l)
        x2d = _ffn(x2d, row(norm_ffn[l]), w_gate_up_b, w_down_b, tm_big, tf, l)
    return x2d.reshape(batch, seq, d)
```

```python
import math

import jax
import jax.numpy as jnp
from jax import lax
from jax.experimental import pallas as pl
from jax.experimental.pallas import tpu as pltpu

HEAD_DIM = 128
N_Q_HEADS = 8
N_KV_HEADS = 2
GQA_GROUP = N_Q_HEADS // N_KV_HEADS
N_MEM_HEADS = 4
SGU_GROUPS = 4
SGU_CHUNK = 128
GRID_W = 64
ROPE_THETA = 10000.0
N_BRANCH = 3
EPS = 1e-6

SGU_WIDTH = SGU_GROUPS * HEAD_DIM
Q_WIDTH = N_Q_HEADS * HEAD_DIM
KV_WIDTH = N_KV_HEADS * HEAD_DIM
MEM_WIDTH = N_MEM_HEADS * HEAD_DIM
MIX_WIDTH = 2 * SGU_WIDTH + Q_WIDTH + 2 * KV_WIDTH + MEM_WIDTH
OFF_Q = 2 * SGU_WIDTH
OFF_K = OFF_Q + Q_WIDTH
OFF_V = OFF_K + KV_WIDTH
OFF_MQ = OFF_V + KV_WIDTH

LOG2E = math.log2(math.e)
ATTN_SCALE = HEAD_DIM ** -0.5

V7X_VMEM_LIMIT_BYTES = 56 * 1024 * 1024
SUBLANES = 8

INPROJ_ROWS = 512
DENSE_ROWS = 1024
WEIGHT_COLS = 512
FLASH_ROWS = 2048
ONLINE_Q_ROWS = 512
ONLINE_K_ROWS = 1024

STAB_SLACK = 60.0
ROW_SUM_FLOOR = 2.0 ** -80
KEY_NORM_MARGIN = 1.01

BF16 = jnp.bfloat16
F32 = jnp.float32


def _dot(a, b):
    return jnp.dot(a, b, preferred_element_type=F32)


def _dot_nt(a, b):
    return lax.dot_general(a, b, (((1,), (1,)), ((), ())), preferred_element_type=F32)


def _rms(z, gain):
    return z * lax.rsqrt(jnp.mean(z * z, axis=-1, keepdims=True) + EPS) * gain


def _gelu(z):
    return 0.5 * z * (1.0 + lax.erf(z * math.sqrt(0.5)))


def _rope(z, cos, sin_signed, first_of_pair):
    partner = jnp.where(first_of_pair, pltpu.roll(z, 96, 1), pltpu.roll(z, 32, 1))
    return z * cos + partner * sin_signed


def _params(sem):
    return pltpu.CompilerParams(dimension_semantics=sem, vmem_limit_bytes=V7X_VMEM_LIMIT_BYTES)


def _memkv_kernel(mem_ref, mnorm_ref, w_ref, mkn_ref, mk_ref, mv_ref):
    hm = _rms(mem_ref[...], mnorm_ref[...]).astype(BF16)
    kv = _dot(hm, w_ref[...])
    for hh in range(N_MEM_HEADS):
        sl = slice(hh * HEAD_DIM, (hh + 1) * HEAD_DIM)
        mk_ref[:, sl] = _rms(kv[:, sl], mkn_ref[...]).astype(BF16)
    mv_ref[...] = kv[:, MEM_WIDTH:].astype(BF16)


def _memkv(mem2d, mem_norm, w_mem_kv, mk_norm, n_mem, layer):
    rows, d = mem2d.shape
    return pl.pallas_call(
        _memkv_kernel,
        grid=(rows // n_mem,),
        in_specs=[
            pl.BlockSpec((n_mem, d), lambda i: (i, 0)),
            pl.BlockSpec((1, d), lambda i: (0, 0)),
            pl.BlockSpec((None, d, 2 * MEM_WIDTH), lambda i: (layer, 0, 0)),
            pl.BlockSpec((1, HEAD_DIM), lambda i: (0, 0)),
        ],
        out_specs=[
            pl.BlockSpec((n_mem, MEM_WIDTH), lambda i: (i, 0)),
            pl.BlockSpec((n_mem, MEM_WIDTH), lambda i: (i, 0)),
        ],
        out_shape=[jax.ShapeDtypeStruct((rows, MEM_WIDTH), BF16)] * 2,
        compiler_params=_params(("arbitrary",)),
        name="memkv",
    )(mem2d, mem_norm, w_mem_kv, mk_norm)


def _inproj_kernel(x_ref, nmix_ref, w_ref, cos_ref, sin_ref, sgun_ref, sguw_ref, sgub_ref,
                   qn_ref, kn_ref, mqn_ref, mk_ref, mv_ref,
                   ysgu_ref, q_ref, k_ref, v_ref, ymem_ref, kn2_ref, qlen_ref, h_ref, za_sc, zb_sc):
    i = pl.program_id(0)
    rest = (cos_ref, sin_ref, sgun_ref, sguw_ref, sgub_ref, qn_ref, kn_ref, mqn_ref, mk_ref, mv_ref,
            ysgu_ref, q_ref, k_ref, v_ref, ymem_ref, kn2_ref, qlen_ref)

    @pl.when(i == 0)
    def _():
        zb_sc[...] = jnp.zeros(zb_sc.shape, F32)

    def step(z_write, z_read):
        h = _rms(x_ref[...], nmix_ref[...]).astype(BF16)
        h_ref[...] = h
        z_write[...] = _dot(h, w_ref[...])
        _inproj_finish(z_read, *rest)

    @pl.when(i % 2 == 0)
    def _():
        step(za_sc, zb_sc)

    @pl.when(i % 2 == 1)
    def _():
        step(zb_sc, za_sc)


def _inproj_finish(z, cos_ref, sin_ref, sgun_ref, sguw_ref, sgub_ref, qn_ref, kn_ref, mqn_ref,
                   mk_ref, mv_ref, ysgu_ref, q_ref, k_ref, v_ref, ymem_ref, kn2_ref, qlen_ref):
    tm = z.shape[0]

    zs = _gelu(z[:, 0:OFF_Q])
    for g in range(SGU_GROUPS):
        sl = slice(g * HEAD_DIM, (g + 1) * HEAD_DIM)
        u_g = zs[:, sl]
        vs_g = _rms(zs[:, SGU_WIDTH + g * HEAD_DIM:SGU_WIDTH + (g + 1) * HEAD_DIM],
                    sgun_ref[...]).astype(BF16)
        w_g = sguw_ref[g]
        for c in range(tm // SGU_CHUNK):
            rows = slice(c * SGU_CHUNK, (c + 1) * SGU_CHUNK)
            sp = _dot(w_g, vs_g[rows, :]) + sgub_ref[:, sl]
            ysgu_ref[rows, sl] = (u_g[rows, :] * sp).astype(BF16)

    cos = cos_ref[...]
    sin_signed = sin_ref[...]
    lane = lax.broadcasted_iota(jnp.int32, (tm, HEAD_DIM), 1)
    first_of_pair = (lane & 32) == 0
    zq = z[:, OFF_Q:OFF_K]
    for hh in range(N_Q_HEADS):
        sl = slice(hh * HEAD_DIM, (hh + 1) * HEAD_DIM)
        qh = _rope(_rms(zq[:, sl], qn_ref[...]), cos, sin_signed, first_of_pair)
        qb = (qh * (ATTN_SCALE * LOG2E)).astype(BF16)
        q_ref[:, sl] = qb
        qf = qb.astype(F32)
        qlen_ref[:, sl] = jnp.broadcast_to(jnp.sqrt(jnp.sum(qf * qf, axis=-1, keepdims=True)), (tm, HEAD_DIM))
    zkv = z[:, OFF_K:OFF_MQ]
    for hh in range(N_KV_HEADS):
        sl = slice(hh * HEAD_DIM, (hh + 1) * HEAD_DIM)
        kh = _rope(_rms(zkv[:, sl], kn_ref[...]), cos, sin_signed, first_of_pair)
        k_ref[:, sl] = kh.astype(BF16)
        kn2 = jnp.max(jnp.sum(kh * kh, axis=-1, keepdims=True), axis=0, keepdims=True)
        kn2_ref[:, sl] = jnp.broadcast_to(kn2, (kn2_ref.shape[0], HEAD_DIM))
        v_ref[:, 2 * hh * HEAD_DIM:(2 * hh + 1) * HEAD_DIM] = zkv[:, KV_WIDTH + hh * HEAD_DIM:
                                                                   KV_WIDTH + (hh + 1) * HEAD_DIM].astype(BF16)
        v_ref[:, (2 * hh + 1) * HEAD_DIM:(2 * hh + 2) * HEAD_DIM] = jnp.ones((tm, HEAD_DIM), BF16)

    zmq = z[:, OFF_MQ:MIX_WIDTH]
    for hh in range(N_MEM_HEADS):
        sl = slice(hh * HEAD_DIM, (hh + 1) * HEAD_DIM)
        mq = (_rms(zmq[:, sl], mqn_ref[...]) * (ATTN_SCALE * LOG2E)).astype(BF16)
        s = _dot_nt(mq, mk_ref[:, sl])
        p = jnp.exp2(s - jnp.max(s, axis=-1, keepdims=True))
        o = _dot(p.astype(BF16), mv_ref[:, sl])
        ymem_ref[:, sl] = (o / jnp.sum(p, axis=-1, keepdims=True)).astype(BF16)


def _inproj(x2d, norm_mix, w_in, cos_t, sin_t, sgu_norm, sgu_w, sgu_b_full, q_norm, k_norm,
            mq_norm, mk, mv, seq, n_mem, tm, layer):
    t, d = x2d.shape
    tiles_per_seq = seq // tm
    n_tiles = t // tm
    const = lambda i: (0, 0)
    prev = lambda i: jnp.maximum(i - 1, 0)
    row = lambda i: (prev(i), 0)
    cur = lambda i: (jnp.minimum(i, n_tiles - 1), 0)
    return pl.pallas_call(
        _inproj_kernel,
        grid=(n_tiles + 1,),
        in_specs=[
            pl.BlockSpec((tm, d), cur),
            pl.BlockSpec((1, d), const),
            pl.BlockSpec((None, d, MIX_WIDTH), lambda i: (layer, 0, 0), pipeline_mode=pl.Buffered(1)),
            pl.BlockSpec((tm, HEAD_DIM), lambda i: (prev(i) % tiles_per_seq, 0)),
            pl.BlockSpec((tm, HEAD_DIM), lambda i: (prev(i) % tiles_per_seq, 0)),
            pl.BlockSpec((1, HEAD_DIM), const),
            pl.BlockSpec((None, SGU_GROUPS, SGU_CHUNK, SGU_CHUNK), lambda i: (layer, 0, 0, 0)),
            pl.BlockSpec((SGU_CHUNK, SGU_WIDTH), const),
            pl.BlockSpec((1, HEAD_DIM), const),
            pl.BlockSpec((1, HEAD_DIM), const),
            pl.BlockSpec((1, HEAD_DIM), const),
            pl.BlockSpec((n_mem, MEM_WIDTH), lambda i: (prev(i) // tiles_per_seq, 0)),
            pl.BlockSpec((n_mem, MEM_WIDTH), lambda i: (prev(i) // tiles_per_seq, 0)),
        ],
        out_specs=[
            pl.BlockSpec((tm, SGU_WIDTH), row),
            pl.BlockSpec((tm, Q_WIDTH), row),
            pl.BlockSpec((tm, KV_WIDTH), row),
            pl.BlockSpec((tm, 2 * KV_WIDTH), row),
            pl.BlockSpec((tm, MEM_WIDTH), row),
            pl.BlockSpec((SUBLANES, KV_WIDTH), row),
            pl.BlockSpec((tm, Q_WIDTH), row),
            pl.BlockSpec((tm, d), cur),
        ],
        out_shape=[
            jax.ShapeDtypeStruct((t, SGU_WIDTH), BF16),
            jax.ShapeDtypeStruct((t, Q_WIDTH), BF16),
            jax.ShapeDtypeStruct((t, KV_WIDTH), BF16),
            jax.ShapeDtypeStruct((t, 2 * KV_WIDTH), BF16),
            jax.ShapeDtypeStruct((t, MEM_WIDTH), BF16),
            jax.ShapeDtypeStruct((t // tm * SUBLANES, KV_WIDTH), F32),
            jax.ShapeDtypeStruct((t, Q_WIDTH), F32),
            jax.ShapeDtypeStruct((t, d), BF16),
        ],
        scratch_shapes=[pltpu.VMEM((tm, MIX_WIDTH), F32), pltpu.VMEM((tm, MIX_WIDTH), F32)],
        compiler_params=_params(("arbitrary",)),
        name="inproj",
    )(x2d, norm_mix, w_in, cos_t, sin_t, sgu_norm, sgu_w, sgu_b_full, q_norm, k_norm, mq_norm, mk, mv)


def _flash_kernel(q_ref, k_ref, v_ref, o_ref, m_sc, l_sc, acc_sc):
    ki = pl.program_id(3)

    @pl.when(ki == 0)
    def _():
        m_sc[...] = jnp.full(m_sc.shape, -jnp.inf, F32)
        l_sc[...] = jnp.zeros(l_sc.shape, F32)
        acc_sc[...] = jnp.zeros(acc_sc.shape, F32)

    k = k_ref[...]
    v = v_ref[...]
    for hh in range(GQA_GROUP):
        sl = slice(hh * HEAD_DIM, (hh + 1) * HEAD_DIM)
        s = _dot_nt(q_ref[:, sl], k)
        m_prev = m_sc[hh]
        m_new = jnp.maximum(m_prev, jnp.max(s, axis=-1, keepdims=True))
        alpha = jnp.exp2(m_prev - m_new)
        p = jnp.exp2(s - m_new[:, 0:1])
        l_sc[hh] = alpha * l_sc[hh] + jnp.sum(p, axis=-1, keepdims=True)
        acc_sc[hh] = alpha * acc_sc[hh] + _dot(p.astype(BF16), v)
        m_sc[hh] = m_new

    @pl.when(ki == pl.num_programs(3) - 1)
    def _():
        for hh in range(GQA_GROUP):
            sl = slice(hh * HEAD_DIM, (hh + 1) * HEAD_DIM)
            o_ref[:, sl] = (acc_sc[hh] / l_sc[hh]).astype(BF16)


def _flash_fast_kernel(kmax_ref, q_ref, qn_ref, k_ref, v_ref, o_ref, lmin_ref, acc_sc):
    ki = pl.program_id(3)
    tk = k_ref.shape[0]

    @pl.when(ki == 0)
    def _():
        acc_sc[...] = jnp.zeros(acc_sc.shape, F32)

    kmax = kmax_ref[pl.program_id(0), pl.program_id(1)]
    k = k_ref[...]
    v = v_ref[...]
    for hh in range(GQA_GROUP):
        sl = slice(hh * HEAD_DIM, (hh + 1) * HEAD_DIM)
        s = _dot_nt(q_ref[:, sl], k)
        c = qn_ref[:, sl] * kmax - STAB_SLACK
        p = jnp.exp2(s - jnp.tile(c, (1, tk // HEAD_DIM)))
        acc_sc[hh] += _dot(p.astype(BF16), v)

    @pl.when(ki == pl.num_programs(3) - 1)
    def _():
        lmin = None
        for hh in range(GQA_GROUP):
            acc = acc_sc[hh]
            l = acc[:, HEAD_DIM:]
            o_ref[:, hh * HEAD_DIM:(hh + 1) * HEAD_DIM] = (acc[:, :HEAD_DIM] / l).astype(BF16)
            lm = jnp.min(l, axis=0, keepdims=True)
            lmin = lm if lmin is None else jnp.minimum(lmin, lm)
        lmin_ref[...] = jnp.broadcast_to(lmin, lmin_ref.shape)


def _flash_fast(kmax, q, qn, k, v_ext, batch, seq, tq, tk):
    t = q.shape[0]
    nq = seq // tq
    nk = seq // tk
    group_w = GQA_GROUP * HEAD_DIM
    return pl.pallas_call(
        _flash_fast_kernel,
        grid=(batch, N_KV_HEADS, nq, nk),
        in_specs=[
            pl.BlockSpec(memory_space=pltpu.SMEM),
            pl.BlockSpec((tq, group_w), lambda b, g, qi, ki: (b * nq + qi, g)),
            pl.BlockSpec((tq, group_w), lambda b, g, qi, ki: (b * nq + qi, g)),
            pl.BlockSpec((tk, HEAD_DIM), lambda b, g, qi, ki: (b * nk + ki, g)),
            pl.BlockSpec((tk, 2 * HEAD_DIM), lambda b, g, qi, ki: (b * nk + ki, g)),
        ],
        out_specs=[
            pl.BlockSpec((tq, group_w), lambda b, g, qi, ki: (b * nq + qi, g)),
            pl.BlockSpec((SUBLANES, HEAD_DIM), lambda b, g, qi, ki: (b * nq + qi, g)),
        ],
        out_shape=[
            jax.ShapeDtypeStruct((t, Q_WIDTH), BF16),
            jax.ShapeDtypeStruct((batch * nq * SUBLANES, N_KV_HEADS * HEAD_DIM), F32),
        ],
        scratch_shapes=[pltpu.VMEM((GQA_GROUP, tq, 2 * HEAD_DIM), F32)],
        compiler_params=_params(("arbitrary", "arbitrary", "arbitrary", "arbitrary")),
        name="flash_fast",
    )(kmax, q, qn, k, v_ext)


def _flash(q, k, v_ext, batch, seq, tq, tk):
    t = q.shape[0]
    nq = seq // tq
    nk = seq // tk
    group_w = GQA_GROUP * HEAD_DIM
    return pl.pallas_call(
        _flash_kernel,
        grid=(batch, N_KV_HEADS, nq, nk),
        in_specs=[
            pl.BlockSpec((tq, group_w), lambda b, g, qi, ki: (b * nq + qi, g)),
            pl.BlockSpec((tk, HEAD_DIM), lambda b, g, qi, ki: (b * nk + ki, g)),
            pl.BlockSpec((tk, HEAD_DIM), lambda b, g, qi, ki: (b * nk + ki, 2 * g)),
        ],
        out_specs=pl.BlockSpec((tq, group_w), lambda b, g, qi, ki: (b * nq + qi, g)),
        out_shape=jax.ShapeDtypeStruct((t, Q_WIDTH), BF16),
        scratch_shapes=[
            pltpu.VMEM((GQA_GROUP, tq, HEAD_DIM), F32),
            pltpu.VMEM((GQA_GROUP, tq, HEAD_DIM), F32),
            pltpu.VMEM((GQA_GROUP, tq, HEAD_DIM), F32),
        ],
        compiler_params=_params(("arbitrary", "arbitrary", "arbitrary", "arbitrary")),
        name="flash",
    )(q, k, v_ext)


def _attention(q, qn, k, v_ext, kn2, batch, seq, tm, tq, tk, tq_fast, tk_fast):
    kmax = jnp.sqrt(jnp.max(
        kn2.reshape(batch, seq // tm * SUBLANES, N_KV_HEADS, HEAD_DIM), axis=(1, 3)))
    y_fast, lmin = _flash_fast(kmax * KEY_NORM_MARGIN, q, qn, k, v_ext, batch, seq, tq_fast, tk_fast)
    trusted = jnp.all(lmin >= ROW_SUM_FLOOR)
    return lax.cond(trusted, lambda: y_fast, lambda: _flash(q, k, v_ext, batch, seq, tq, tk))


def _gatemix_kernel(h_ref, ysgu_ref, yattn_ref, ymem_ref, wg0_ref, wg1_ref, wg2_ref, gb_ref,
                    wbs_ref, wba_ref, wbm_ref, y_ref):
    h = h_ref[...]
    y = jax.nn.sigmoid(_dot(h, wg0_ref[...]) + gb_ref[0:1, :]) * _dot(ysgu_ref[...], wbs_ref[...])
    y += jax.nn.sigmoid(_dot(h, wg1_ref[...]) + gb_ref[1:2, :]) * _dot(yattn_ref[...], wba_ref[...])
    y += jax.nn.sigmoid(_dot(h, wg2_ref[...]) + gb_ref[2:3, :]) * _dot(ymem_ref[...], wbm_ref[...])
    y_ref[...] = y.astype(BF16)


def _gatemix(h, y_sgu, y_attn, y_mem, w_in, gate_b, w_br_sgu, w_br_attn, w_br_mem, tm, tn, layer):
    t, d = h.shape
    nj = d // tn
    gate0 = MIX_WIDTH // tn
    row = lambda i, j: (i, 0)
    lcol = lambda i, j: (layer, 0, j)
    return pl.pallas_call(
        _gatemix_kernel,
        grid=(t // tm, nj),
        in_specs=[
            pl.BlockSpec((tm, d), row),
            pl.BlockSpec((tm, SGU_WIDTH), row),
            pl.BlockSpec((tm, Q_WIDTH), row),
            pl.BlockSpec((tm, MEM_WIDTH), row),
            pl.BlockSpec((None, d, tn), lambda i, j: (layer, 0, gate0 + j)),
            pl.BlockSpec((None, d, tn), lambda i, j: (layer, 0, gate0 + nj + j)),
            pl.BlockSpec((None, d, tn), lambda i, j: (layer, 0, gate0 + 2 * nj + j)),
            pl.BlockSpec((N_BRANCH, tn), lambda i, j: (0, j)),
            pl.BlockSpec((None, SGU_WIDTH, tn), lcol),
            pl.BlockSpec((None, Q_WIDTH, tn), lcol),
            pl.BlockSpec((None, MEM_WIDTH, tn), lcol),
        ],
        out_specs=pl.BlockSpec((tm, tn), lambda i, j: (i, j)),
        out_shape=jax.ShapeDtypeStruct((t, d), BF16),
        compiler_params=_params(("arbitrary", "arbitrary")),
        name="gatemix",
    )(h, y_sgu, y_attn, y_mem, w_in, w_in, w_in, gate_b, w_br_sgu, w_br_attn, w_br_mem)


def _outproj_kernel(x_ref, y_ref, w_ref, o_ref):
    o_ref[...] = x_ref[...] + _dot(y_ref[...], w_ref[...])


def _outproj(x2d, y, w_out, tm, layer):
    t, d = x2d.shape
    row = lambda i: (i, 0)
    return pl.pallas_call(
        _outproj_kernel,
        grid=(t // tm,),
        in_specs=[
            pl.BlockSpec((tm, d), row),
            pl.BlockSpec((tm, d), row),
            pl.BlockSpec((None, d, d), lambda i: (layer, 0, 0), pipeline_mode=pl.Buffered(1)),
        ],
        out_specs=pl.BlockSpec((tm, d), row),
        out_shape=jax.ShapeDtypeStruct((t, d), F32),
        compiler_params=_params(("arbitrary",)),
        name="outproj",
    )(x2d, y, w_out)


def _ffn_kernel(x_ref, nffn_ref, wg_ref, wu_ref, wd_ref, o_ref, h_sc):
    j = pl.program_id(1)

    @pl.when(j == 0)
    def _():
        x = x_ref[...]
        h_sc[...] = _rms(x, nffn_ref[...]).astype(BF16)
        o_ref[...] = x

    h = h_sc[...]
    a = _dot(h, wg_ref[...])
    b = _dot(h, wu_ref[...])
    o_ref[...] += _dot((a * jax.nn.sigmoid(a) * b).astype(BF16), wd_ref[...])


def _ffn(x2d, norm_ffn, w_gate_up, w_down, tm, tf, layer):
    t, d = x2d.shape
    d_ff = w_down.shape[1]
    nj = d_ff // tf
    row = lambda i, j: (i, 0)
    return pl.pallas_call(
        _ffn_kernel,
        grid=(t // tm, nj),
        in_specs=[
            pl.BlockSpec((tm, d), row),
            pl.BlockSpec((1, d), lambda i, j: (0, 0)),
            pl.BlockSpec((None, d, tf), lambda i, j: (layer, 0, j)),
            pl.BlockSpec((None, d, tf), lambda i, j: (layer, 0, nj + j)),
            pl.BlockSpec((None, tf, d), lambda i, j: (layer, j, 0)),
        ],
        out_specs=pl.BlockSpec((tm, d), row),
        out_shape=jax.ShapeDtypeStruct((t, d), F32),
        scratch_shapes=[pltpu.VMEM((tm, d), BF16)],
        compiler_params=_params(("arbitrary", "arbitrary")),
        name="ffn",
    )(x2d, norm_ffn, w_gate_up, w_gate_up, w_down)


def _rope_tables(seq):
    quarter = HEAD_DIM // 4
    inv = ROPE_THETA ** (-jnp.arange(0, 2 * quarter, 2, dtype=F32) / (2 * quarter))
    rows = seq // GRID_W
    ar = jnp.arange(rows, dtype=F32)[:, None] * inv[None, :]
    ac = jnp.arange(GRID_W, dtype=F32)[:, None] * inv[None, :]
    by_row = lambda a: jnp.repeat(a, GRID_W, axis=0)
    by_col = lambda a: jnp.tile(a, (rows, 1))
    cr, sr, cc, sc = by_row(jnp.cos(ar)), by_row(jnp.sin(ar)), by_col(jnp.cos(ac)), by_col(jnp.sin(ac))
    cos_t = jnp.concatenate([cr, cr, cc, cc], axis=-1)
    sin_t = jnp.concatenate([-sr, sr, -sc, sc], axis=-1)
    return cos_t, sin_t


def _tile(n, want):
    if n <= want:
        return n
    for cand in range(want, 0, -128):
        if n % cand == 0:
            return cand
    return n


def kernel(x, mem, norm_mix, w_in, gate_b, sgu_norm, sgu_w, sgu_b, q_norm, k_norm, mem_norm,
           w_mem_kv, mq_norm, mk_norm, w_br_sgu, w_br_attn, w_br_mem, w_out, norm_ffn,
           w_gate_up, w_down):
    batch, seq, d = x.shape
    n_mem = mem.shape[1]
    depth = w_in.shape[0]
    assert seq % SGU_CHUNK == 0 and seq % GRID_W == 0

    tm = _tile(seq, INPROJ_ROWS)
    tq = _tile(seq, ONLINE_Q_ROWS)
    tk = _tile(seq, ONLINE_K_ROWS)
    tq_fast = _tile(seq, FLASH_ROWS)
    tk_fast = _tile(seq, FLASH_ROWS)
    tn = _tile(d, WEIGHT_COLS)
    tm_big = _tile(seq, DENSE_ROWS)
    tf = _tile(w_down.shape[1], WEIGHT_COLS)

    cos_t, sin_t = _rope_tables(seq)
    x2d = x.reshape(batch * seq, d)
    mem2d = mem.reshape(batch * n_mem, d)
    row = lambda a: a.reshape(1, -1)

    w_in_b, w_mem_kv_b, sgu_w_b = w_in.astype(BF16), w_mem_kv.astype(BF16), sgu_w.astype(BF16)
    w_br_sgu_b, w_br_attn_b, w_br_mem_b = (w.astype(BF16) for w in (w_br_sgu, w_br_attn, w_br_mem))
    w_out_b, w_gate_up_b, w_down_b = w_out.astype(BF16), w_gate_up.astype(BF16), w_down.astype(BF16)

    for l in range(depth):
        sgu_b_full = jnp.repeat(sgu_b[l].T, HEAD_DIM, axis=1)
        mk, mv = _memkv(mem2d, row(mem_norm[l]), w_mem_kv_b, row(mk_norm[l]), n_mem, l)
        y_sgu, q, k, v_ext, y_mem, kn2, qn, h = _inproj(
            x2d, row(norm_mix[l]), w_in_b, cos_t, sin_t, row(sgu_norm[l]), sgu_w_b, sgu_b_full,
            row(q_norm[l]), row(k_norm[l]), row(mq_norm[l]), mk, mv, seq, n_mem, tm, l)
        y_attn = _attention(q, qn, k, v_ext, kn2, batch, seq, tm, tq, tk, tq_fast, tk_fast)
        y = _gatemix(h, y_sgu, y_attn, y_mem, w_in_b, gate_b[l], w_br_sgu_b, w_br_attn_b, w_br_mem_b,
                     tm_big, tn, l)
        x2d = _outproj(x2d, y, w_out_b, tm_big, l)
        x2d = _ffn(x2d, row(norm_ffn[l]), w_gate_up_b, w_down_b, tm_big, tf, l)
    return x2d.reshape(batch, seq, d)
```
